```python
import math
import jax, jax.numpy as jnp
from jax import lax
import numpy as np

D_MODEL = 2048
BATCH = 2
SEQ = 16384
DEPTH = 1

MLA_HEADS = 8
Q_LORA_RANK = 512
KV_LORA_RANK = 512
QK_NOPE_DIM = 128
QK_ROPE_DIM = 64
V_HEAD_DIM = 128
ROPE_THETA = 10000.0
DIL_PAIRS = ((128, 1), (512, 4), (2048, 16))
DIL_GROUPS = len(DIL_PAIRS)
DIL_HEADS = 8
DIL_HEAD_DIM = 128
REL_BUCKETS = 32
REL_MAX_DISTANCE = 2048
REL_HEADS = DIL_GROUPS * DIL_HEADS
Q_BLOCK = 128
N_GROUPS = 8
EXPERTS_PER_GROUP = 8
N_EXPERTS = N_GROUPS * EXPERTS_PER_GROUP
TOP_K = 2
EXPERT_FF = 1408
MOE_BLOCK = 256
N_MODS = 6
EPS = 1e-6

MLA_Q_COLS = MLA_HEADS * (QK_NOPE_DIM + QK_ROPE_DIM)
MLA_KV_COLS = MLA_HEADS * (QK_NOPE_DIM + V_HEAD_DIM)
DIL_COLS = DIL_GROUPS * DIL_HEADS * DIL_HEAD_DIM
IN_SPLITS = (Q_LORA_RANK, KV_LORA_RANK, QK_ROPE_DIM, DIL_COLS, DIL_COLS, DIL_COLS, D_MODEL, D_MODEL)
IN_COLS = sum(IN_SPLITS)

kernel_name = "hybrid_mla_dilated_hmoe_block"


def rms_norm(x, g):
    xf = x.astype(jnp.float32)
    y = xf * lax.rsqrt(jnp.mean(xf * xf, axis=-1, keepdims=True) + EPS)
    return (y * g.astype(jnp.float32)).astype(x.dtype)


def rope_tables(positions):
    half = QK_ROPE_DIM // 2
    inv_freq = 1.0 / (ROPE_THETA ** (jnp.arange(half, dtype=jnp.float32) / half))
    ang = positions.astype(jnp.float32)[..., None] * inv_freq
    return jnp.cos(ang), jnp.sin(ang)


def apply_rope(x, cos, sin):
    xf = x.astype(jnp.float32)
    x1, x2 = jnp.split(xf, 2, axis=-1)
    return jnp.concatenate([x1 * cos - x2 * sin, x1 * sin + x2 * cos], axis=-1).astype(x.dtype)


def t5_bucket(n):
    exact = REL_BUCKETS // 2
    nf = jnp.maximum(n, 1).astype(jnp.float32)
    large = exact + (jnp.log(nf / exact) / math.log(REL_MAX_DISTANCE / exact)
                     * (REL_BUCKETS - exact)).astype(jnp.int32)
    large = jnp.minimum(large, REL_BUCKETS - 1)
    return jnp.where(n < exact, n, large)


def mla_attention(q_nope, q_rope, k_nope, k_rope, v):
    B, S = q_nope.shape[:2]
    scale = (QK_NOPE_DIM + QK_ROPE_DIM) ** -0.5
    kpos = jnp.arange(S, dtype=jnp.int32)

    def block(qs):
        qn = lax.dynamic_slice_in_dim(q_nope, qs, Q_BLOCK, axis=1)
        qr = lax.dynamic_slice_in_dim(q_rope, qs, Q_BLOCK, axis=1)
        s = (jnp.einsum('bqhd,bkhd->bhqk', qn, k_nope, preferred_element_type=jnp.float32)
             + jnp.einsum('bqhr,bkr->bhqk', qr, k_rope, preferred_element_type=jnp.float32)) * scale
        qpos = qs + jnp.arange(Q_BLOCK, dtype=jnp.int32)
        s = jnp.where(kpos[None, :] <= qpos[:, None], s, -jnp.inf)
        p = jax.nn.softmax(s, axis=-1)
        return jnp.einsum('bhqk,bkhd->bqhd', p.astype(v.dtype), v)

    out = lax.map(block, jnp.arange(S // Q_BLOCK, dtype=jnp.int32) * Q_BLOCK)
    return out.transpose(1, 0, 2, 3, 4).reshape(B, S, MLA_HEADS * V_HEAD_DIM)


def dilated_attention(q, k, v, rel_bias):
    B, S = q.shape[:2]
    scale = DIL_HEAD_DIM ** -0.5
    groups = []
    for g, (window, dil) in enumerate(DIL_PAIRS):
        offs = dil * jnp.arange(window // dil + 1, dtype=jnp.int32)
        bias = rel_bias[t5_bucket(offs)][:, g * DIL_HEADS:(g + 1) * DIL_HEADS]
        groups.append((q[:, :, g], k[:, :, g], v[:, :, g], offs, bias.T.astype(jnp.float32)))

    def block(qs):
        tq = qs + jnp.arange(Q_BLOCK, dtype=jnp.int32)
        outs, lses = [], []
        for qg, kg, vg, offs, bias in groups:
            n_keys = offs.shape[0]
            idx = tq[:, None] - offs[None, :]
            valid = idx >= 0
            flat = jnp.maximum(idx, 0).reshape(-1)
            kj = jnp.take(kg, flat, axis=1).reshape(B, Q_BLOCK, n_keys, DIL_HEADS, DIL_HEAD_DIM)
            vj = jnp.take(vg, flat, axis=1).reshape(B, Q_BLOCK, n_keys, DIL_HEADS, DIL_HEAD_DIM)
            qb = lax.dynamic_slice_in_dim(qg, qs, Q_BLOCK, axis=1)
            s = jnp.einsum('bqhd,bqjhd->bhqj', qb, kj, preferred_element_type=jnp.float32) * scale
            s = jnp.where(valid[None, None], s + bias[None, :, None, :], -jnp.inf)
            m = jnp.max(s, axis=-1, keepdims=True)
            p = jnp.exp(s - m)
            den = jnp.sum(p, axis=-1)
            o = jnp.einsum('bhqj,bqjhd->bqhd', p.astype(vj.dtype), vj,
                           preferred_element_type=jnp.float32) / den.transpose(0, 2, 1)[..., None]
            outs.append(o)
            lses.append(m[..., 0] + jnp.log(den))
        wts = jax.nn.softmax(jnp.stack(lses), axis=0)
        o = jnp.einsum('gbhq,gbqhd->bqhd', wts, jnp.stack(outs))
        return o.astype(v.dtype)

    out = lax.map(block, jnp.arange(S // Q_BLOCK, dtype=jnp.int32) * Q_BLOCK)
    return out.transpose(1, 0, 2, 3, 4).reshape(B, S, DIL_HEADS * DIL_HEAD_DIM)


def token_mixer(h, cos, sin, w_in, q_norm_g, w_uq, kv_norm_g, w_ukv, rel_bias,
                w_o_mla, w_o_dil, w_out):
    B, S, _ = h.shape
    proj = jnp.einsum('bsd,dc->bsc', h, w_in)
    split_at = np.cumsum(IN_SPLITS)[:-1].tolist()
    c_q, c_kv, k_r, q_d, k_d, v_d, g_a, g_b = jnp.split(proj, split_at, axis=-1)
    q = jnp.einsum('bsr,rc->bsc', rms_norm(c_q, q_norm_g), w_uq)
    q = q.reshape(B, S, MLA_HEADS, QK_NOPE_DIM + QK_ROPE_DIM)
    q_nope, q_rope = q[..., :QK_NOPE_DIM], q[..., QK_NOPE_DIM:]
    kv = jnp.einsum('bsr,rc->bsc', rms_norm(c_kv, kv_norm_g), w_ukv)
    kv = kv.reshape(B, S, MLA_HEADS, QK_NOPE_DIM + V_HEAD_DIM)
    k_nope, v_a = kv[..., :QK_NOPE_DIM], kv[..., QK_NOPE_DIM:]
    q_rope = apply_rope(q_rope, cos[:, :, None, :], sin[:, :, None, :])
    k_rope = apply_rope(k_r, cos, sin)
    o_a = jnp.einsum('bsk,kd->bsd', mla_attention(q_nope, q_rope, k_nope, k_rope, v_a), w_o_mla)
    shp = (B, S, DIL_GROUPS, DIL_HEADS, DIL_HEAD_DIM)
    o_b = dilated_attention(q_d.reshape(shp), k_d.reshape(shp), v_d.reshape(shp), rel_bias)
    o_b = jnp.einsum('bsk,kd->bsd', o_b, w_o_dil)
    merged = jax.nn.sigmoid(g_a) * o_a + jax.nn.sigmoid(g_b) * o_b
    return jnp.einsum('bsd,de->bse', merged, w_out)


def hier_moe(h, w_group, b_group, w_expert, b_expert, w_gate, w_up, w_down):
    B, S, D = h.shape
    T = B * S
    hf = h.reshape(T, D)
    g_logits = jnp.einsum('td,dg->tg', hf, w_group).astype(jnp.float32) + b_group.astype(jnp.float32)
    _, g_idx = lax.top_k(g_logits, 1)
    p_group = jnp.take_along_axis(jax.nn.softmax(g_logits, axis=-1), g_idx, axis=1)[:, 0]
    e_logits = (jnp.einsum('td,de->te', hf, w_expert).astype(jnp.float32)
                + b_expert.astype(jnp.float32)).reshape(T, N_GROUPS, EXPERTS_PER_GROUP)
    e_in = jnp.take_along_axis(e_logits, g_idx[:, :, None], axis=1)[:, 0]
    top_v, top_i = lax.top_k(e_in, TOP_K)
    expert_id = g_idx * EXPERTS_PER_GROUP + top_i
    weights = p_group[:, None] * jax.nn.softmax(top_v, axis=-1)
    A = T * TOP_K
    e_flat = expert_id.reshape(A)
    t_flat = jnp.repeat(jnp.arange(T, dtype=jnp.int32), TOP_K)
    w_flat = weights.reshape(A)
    order = jnp.argsort(e_flat)
    e_s, t_s, w_s = e_flat[order], t_flat[order], w_flat[order]
    counts = jax.ops.segment_sum(jnp.ones((A,), jnp.int32), e_flat, num_segments=N_EXPERTS)
    starts = jnp.cumsum(counts) - counts
    padded = (counts + MOE_BLOCK - 1) // MOE_BLOCK * MOE_BLOCK
    pends = jnp.cumsum(padded)
    pstarts = pends - padded
    dest = pstarts[e_s] + (jnp.arange(A, dtype=jnp.int32) - starts[e_s])
    n_blocks = -(-A // MOE_BLOCK) + N_EXPERTS
    P = n_blocks * MOE_BLOCK
    buf_tok = jnp.zeros((P,), jnp.int32).at[dest].set(t_s)
    buf_w = jnp.zeros((P,), jnp.float32).at[dest].set(w_s)
    block_exp = jnp.clip(jnp.searchsorted(pends, jnp.arange(n_blocks, dtype=jnp.int32) * MOE_BLOCK,
                                          side='right'), 0, N_EXPERTS - 1).astype(jnp.int32)
    x_buf = hf[buf_tok].reshape(n_blocks, MOE_BLOCK, D)

    def expert_block(args):
        xb, e = args
        g = xb @ w_gate[e]
        u = xb @ w_up[e]
        return (jax.nn.silu(g) * u) @ w_down[e]

    y_buf = lax.map(expert_block, (x_buf, block_exp)).reshape(P, D)
    y = jnp.zeros((T, D), hf.dtype).at[buf_tok].add((y_buf * buf_w[:, None]).astype(hf.dtype))
    return y.reshape(B, S, D)


def setup_inputs(seed: int = 0) -> dict:
    key = jax.random.key(seed)
    ks = jax.random.split(key, 24)
    f32 = jnp.float32
    nrm = lambda k, shape, fan_in: jax.random.normal(k, shape, f32) * (fan_in ** -0.5)
    gain = lambda k, shape: 1.0 + 0.02 * jax.random.normal(k, shape, f32)
    L = DEPTH
    return {
        "x": jax.random.normal(ks[0], (BATCH, SEQ, D_MODEL), f32),
        "c": jax.random.normal(ks[1], (BATCH, D_MODEL), f32),
        "positions": (jnp.arange(SEQ, dtype=jnp.int32)[None, :]
                      + jax.random.randint(ks[2], (BATCH, 1), 0, 4096, dtype=jnp.int32)),
        "mod_w": nrm(ks[3], (L, D_MODEL, N_MODS * D_MODEL), D_MODEL),
        "mod_b": 0.02 * jax.random.normal(ks[4], (L, N_MODS * D_MODEL), f32),
        "norm1_g": gain(ks[5], (L, D_MODEL)),
        "w_in": nrm(ks[6], (L, D_MODEL, IN_COLS), D_MODEL),
        "q_norm_g": gain(ks[7], (L, Q_LORA_RANK)),
        "w_uq": nrm(ks[8], (L, Q_LORA_RANK, MLA_Q_COLS), Q_LORA_RANK),
        "kv_norm_g": gain(ks[9], (L, KV_LORA_RANK)),
        "w_ukv": nrm(ks[10], (L, KV_LORA_RANK, MLA_KV_COLS), KV_LORA_RANK),
        "rel_bias": 0.5 * jax.random.normal(ks[11], (REL_BUCKETS, REL_HEADS), f32),
        "w_o_mla": nrm(ks[12], (L, MLA_HEADS * V_HEAD_DIM, D_MODEL), MLA_HEADS * V_HEAD_DIM),
        "w_o_dil": nrm(ks[13], (L, DIL_HEADS * DIL_HEAD_DIM, D_MODEL), DIL_HEADS * DIL_HEAD_DIM),
        "w_out": nrm(ks[14], (L, D_MODEL, D_MODEL), D_MODEL),
        "norm2_g": gain(ks[15], (L, D_MODEL)),
        "w_group": nrm(ks[16], (L, D_MODEL, N_GROUPS), D_MODEL),
        "b_group": 0.01 * jax.random.normal(ks[17], (L, N_GROUPS), f32),
        "w_expert": nrm(ks[18], (L, D_MODEL, N_EXPERTS), D_MODEL),
        "b_expert": 0.01 * jax.random.normal(ks[19], (L, N_EXPERTS), f32),
        "w_gate": nrm(ks[20], (L, N_EXPERTS, D_MODEL, EXPERT_FF), D_MODEL),
        "w_up": nrm(ks[21], (L, N_EXPERTS, D_MODEL, EXPERT_FF), D_MODEL),
        "w_down": nrm(ks[22], (L, N_EXPERTS, EXPERT_FF, D_MODEL), EXPERT_FF),
        "final_g": gain(ks[23], (D_MODEL,)),
    }


def reference(x, c, positions, mod_w, mod_b, norm1_g, w_in, q_norm_g, w_uq, kv_norm_g, w_ukv,
              rel_bias, w_o_mla, w_o_dil, w_out, norm2_g, w_group, b_group, w_expert, b_expert,
              w_gate, w_up, w_down, final_g):
    cos, sin = rope_tables(positions)
    c_act = jax.nn.silu(c)
    for l in range(DEPTH):
        mods = jnp.einsum('bd,dm->bm', c_act, mod_w[l]) + mod_b[l]
        sh1, sc1, gt1, sh2, sc2, gt2 = [m[:, None, :] for m in jnp.split(mods, N_MODS, axis=-1)]
        h = rms_norm(x, norm1_g[l]) * (1 + sc1) + sh1
        x = x + gt1 * token_mixer(h, cos, sin, w_in[l], q_norm_g[l], w_uq[l], kv_norm_g[l],
                                  w_ukv[l], rel_bias, w_o_mla[l], w_o_dil[l], w_out[l])
        h = rms_norm(x, norm2_g[l]) * (1 + sc2) + sh2
        x = x + gt2 * hier_moe(h, w_group[l], b_group[l], w_expert[l], b_expert[l],
                               w_gate[l], w_up[l], w_down[l])
    return rms_norm(x, final_g)
```

```python
import functools
import math

import jax
import jax.numpy as jnp
import numpy as np
from jax import lax
from jax.experimental import pallas as pl
from jax.experimental.pallas import tpu as pltpu

F32 = jnp.float32
BF16 = jnp.bfloat16

D_MODEL = 2048
MLA_HEADS = 8
Q_LORA_RANK = 512
KV_LORA_RANK = 512
QK_NOPE_DIM = 128
QK_ROPE_DIM = 64
V_HEAD_DIM = 128
ROPE_THETA = 10000.0
DIL_PAIRS = ((128, 1), (512, 4), (2048, 16))
DIL_GROUPS = len(DIL_PAIRS)
DIL_HEADS = 8
DIL_HEAD_DIM = 128
DIL_KEYS = 129
REL_BUCKETS = 32
REL_MAX_DISTANCE = 2048
N_GROUPS = 8
EXPERTS_PER_GROUP = 8
N_EXPERTS = N_GROUPS * EXPERTS_PER_GROUP
TOP_K = 2
EXPERT_FF = 1408
N_MODS = 6
EPS = 1e-6

LANES = 128
MLA_QK_PAD = 256
DIL_COLS = DIL_GROUPS * DIL_HEADS * DIL_HEAD_DIM
GATE_COLS = 2 * D_MODEL
W1_COLS = GATE_COLS + 3 * DIL_COLS
W1_TILE = 1024
MASK_VALUE = -1e30
VMEM_LIMIT = 56 * 1024 * 1024

SH1, SC1, GT1, SH2, SC2, GT2 = range(N_MODS)


def _params(sem, vmem=VMEM_LIMIT):
    return pltpu.CompilerParams(dimension_semantics=sem, vmem_limit_bytes=vmem)


def _norm_modulate(x, g, scale, shift):
    ms = jnp.mean(x * x, axis=-1, keepdims=True)
    return x * lax.rsqrt(ms + EPS) * g * (1.0 + scale) + shift


def _rms(x, g):
    return x * lax.rsqrt(jnp.mean(x * x, axis=-1, keepdims=True) + EPS) * g


def _mods_kernel(c_ref, w_ref, b_ref, o_ref):
    c = c_ref[...]
    act = c * jax.nn.sigmoid(c)
    o_ref[...] = jnp.dot(act, w_ref[...], preferred_element_type=F32,
                         precision=lax.Precision.HIGHEST) + b_ref[...]


def _mods(c, mod_w, mod_b):
    B = c.shape[0]
    rows = 8
    c_pad = jnp.zeros((rows, D_MODEL), F32).at[:B].set(c)
    n = mod_w.shape[1]
    tn = 1536
    out = pl.pallas_call(
        _mods_kernel,
        grid=(n // tn,),
        in_specs=[pl.BlockSpec((rows, D_MODEL), lambda j: (0, 0)),
                  pl.BlockSpec((D_MODEL, tn), lambda j: (0, j)),
                  pl.BlockSpec((1, tn), lambda j: (0, j))],
        out_specs=pl.BlockSpec((rows, tn), lambda j: (0, j)),
        out_shape=jax.ShapeDtypeStruct((rows, n), F32),
        compiler_params=_params(("arbitrary",)),
    )(c_pad, mod_w, mod_b.reshape(1, n))
    return out[:B].reshape(B, N_MODS, D_MODEL)


def _inproj_kernel(x_ref, mods_ref, g_ref, w_ref, o_ref, h_scr, *, gate_tiles):
    j = pl.program_id(1)

    @pl.when(j == 0)
    def _():
        h = _norm_modulate(x_ref[...], g_ref[...], mods_ref[0, SC1:SC1 + 1, :],
                           mods_ref[0, SH1:SH1 + 1, :])
        h_scr[...] = h.astype(BF16)

    acc = jnp.dot(h_scr[...], w_ref[...], preferred_element_type=F32)

    @pl.when(j < gate_tiles)
    def _():
        o_ref[...] = jax.nn.sigmoid(acc).astype(BF16)

    @pl.when(j >= gate_tiles)
    def _():
        o_ref[...] = acc.astype(BF16)


def _inproj(x2d, mods, norm_g, w1, seq):
    T = x2d.shape[0]
    tm = 1024
    tiles_per_batch = seq // tm
    return pl.pallas_call(
        functools.partial(_inproj_kernel, gate_tiles=GATE_COLS // W1_TILE),
        grid=(T // tm, W1_COLS // W1_TILE),
        in_specs=[pl.BlockSpec((tm, D_MODEL), lambda i, j: (i, 0)),
                  pl.BlockSpec((1, N_MODS, D_MODEL), lambda i, j: (i // tiles_per_batch, 0, 0)),
                  pl.BlockSpec((1, D_MODEL), lambda i, j: (0, 0)),
                  pl.BlockSpec((D_MODEL, W1_TILE), lambda i, j: (0, j))],
        out_specs=pl.BlockSpec((tm, W1_TILE), lambda i, j: (i, j)),
        out_shape=jax.ShapeDtypeStruct((T, W1_COLS), BF16),
        scratch_shapes=[pltpu.VMEM((tm, D_MODEL), BF16)],
        compiler_params=_params(("arbitrary", "arbitrary")),
    )(x2d, mods, norm_g.reshape(1, D_MODEL), w1)


def _mla_prep_kernel(x_ref, mods_ref, g_ref, wm_ref, qg_ref, wuq_ref, kvg_ref, wukv_ref,
                     cos_ref, sin_ref, q_ref, k_ref, v_ref, *, scale):
    h = _norm_modulate(x_ref[...], g_ref[...], mods_ref[0, SC1:SC1 + 1, :],
                       mods_ref[0, SH1:SH1 + 1, :]).astype(BF16)
    cm = jnp.dot(h, wm_ref[...], preferred_element_type=F32)
    cos = cos_ref[0]
    sin = sin_ref[0]
    cq = _rms(cm[:, :Q_LORA_RANK], qg_ref[...]).astype(BF16)
    qx = jnp.dot(cq, wuq_ref[...], preferred_element_type=F32)
    ckv = _rms(cm[:, Q_LORA_RANK:Q_LORA_RANK + KV_LORA_RANK], kvg_ref[...]).astype(BF16)
    kvx = jnp.dot(ckv, wukv_ref[...], preferred_element_type=F32)
    kr0 = Q_LORA_RANK + KV_LORA_RANK
    k_rope = (cm[:, kr0:kr0 + LANES] * cos + cm[:, kr0 + LANES:kr0 + 2 * LANES] * sin).astype(BF16)
    hn = MLA_HEADS * QK_NOPE_DIM
    for hd in range(MLA_HEADS):
        lo = hd * LANES
        q_rope = qx[:, hn + lo:hn + lo + LANES] * cos + qx[:, 2 * hn + lo:2 * hn + lo + LANES] * sin
        q_ref[0, hd, :, :LANES] = (qx[:, lo:lo + LANES] * scale).astype(BF16)
        q_ref[0, hd, :, LANES:] = (q_rope * scale).astype(BF16)
        k_ref[0, hd, :, :LANES] = kvx[:, lo:lo + LANES].astype(BF16)
        k_ref[0, hd, :, LANES:] = k_rope
        v_ref[0, hd] = kvx[:, hn + lo:hn + lo + LANES].astype(BF16)


def _mla_prep(x, mods, norm_g, w_mla, q_norm_g, w_uq_ext, kv_norm_g, w_ukv_r, cos_t, sin_t):
    B, S, _ = x.shape
    tm = 512
    const2 = lambda b, i: (0, 0)
    scale = (QK_NOPE_DIM + QK_ROPE_DIM) ** -0.5
    return pl.pallas_call(
        functools.partial(_mla_prep_kernel, scale=scale),
        grid=(B, S // tm),
        in_specs=[pl.BlockSpec((None, tm, D_MODEL), lambda b, i: (b, i, 0)),
                  pl.BlockSpec((1, N_MODS, D_MODEL), lambda b, i: (b, 0, 0)),
                  pl.BlockSpec((1, D_MODEL), const2),
                  pl.BlockSpec(w_mla.shape, const2),
                  pl.BlockSpec((1, Q_LORA_RANK), const2),
                  pl.BlockSpec(w_uq_ext.shape, const2),
                  pl.BlockSpec((1, KV_LORA_RANK), const2),
                  pl.BlockSpec(w_ukv_r.shape, const2),
                  pl.BlockSpec((1, tm, LANES), lambda b, i: (b, i, 0)),
                  pl.BlockSpec((1, tm, LANES), lambda b, i: (b, i, 0))],
        out_specs=[pl.BlockSpec((1, MLA_HEADS, tm, MLA_QK_PAD), lambda b, i: (b, 0, i, 0)),
                   pl.BlockSpec((1, MLA_HEADS, tm, MLA_QK_PAD), lambda b, i: (b, 0, i, 0)),
                   pl.BlockSpec((1, MLA_HEADS, tm, V_HEAD_DIM), lambda b, i: (b, 0, i, 0))],
        out_shape=[jax.ShapeDtypeStruct((B, MLA_HEADS, S, MLA_QK_PAD), BF16),
                   jax.ShapeDtypeStruct((B, MLA_HEADS, S, MLA_QK_PAD), BF16),
                   jax.ShapeDtypeStruct((B, MLA_HEADS, S, V_HEAD_DIM), BF16)],
        compiler_params=_params(("arbitrary", "arbitrary")),
    )(x, mods, norm_g.reshape(1, D_MODEL), w_mla, q_norm_g.reshape(1, -1), w_uq_ext,
      kv_norm_g.reshape(1, -1), w_ukv_r, cos_t, sin_t)


_NT = (((1,), (1,)), ((), ()))


def _mla_attn_kernel(q_ref, k_ref, v_ref, o_ref, *, blk):
    i = pl.program_id(2)
    q = q_ref[0, 0]

    def step(j, carry, masked):
        m, l, acc = carry
        start = pl.multiple_of(j * blk, blk)
        k = k_ref[0, 0, pl.ds(start, blk), :]
        v = v_ref[0, 0, pl.ds(start, blk), :]
        s = lax.dot_general(q, k, _NT, preferred_element_type=F32)
        if masked:
            row = lax.broadcasted_iota(jnp.int32, (blk, blk), 0)
            col = lax.broadcasted_iota(jnp.int32, (blk, blk), 1)
            s = jnp.where(col <= row, s, MASK_VALUE)
        m_new = jnp.maximum(m, jnp.max(s, axis=-1, keepdims=True))
        alpha = jnp.exp(m - m_new)
        p = jnp.exp(s - m_new)
        l = alpha * l + jnp.sum(p, axis=-1, keepdims=True)
        acc = alpha * acc + jnp.dot(p.astype(BF16), v, preferred_element_type=F32)
        return m_new, l, acc

    init = (jnp.full((blk, 1), MASK_VALUE, F32), jnp.zeros((blk, 1), F32),
            jnp.zeros((blk, V_HEAD_DIM), F32))
    carry = lax.fori_loop(0, i, functools.partial(step, masked=False), init)
    _, l, acc = step(i, carry, True)
    o_ref[0] = (acc / l).astype(BF16)


def _mla_attn(q, k, v):
    B, H, S, _ = q.shape
    blk = 512
    return pl.pallas_call(
        functools.partial(_mla_attn_kernel, blk=blk),
        grid=(B, H, S // blk),
        in_specs=[pl.BlockSpec((1, 1, blk, MLA_QK_PAD), lambda b, h, i: (b, h, i, 0)),
                  pl.BlockSpec((1, 1, S, MLA_QK_PAD), lambda b, h, i: (b, h, 0, 0)),
                  pl.BlockSpec((1, 1, S, V_HEAD_DIM), lambda b, h, i: (b, h, 0, 0))],
        out_specs=pl.BlockSpec((1, blk, V_HEAD_DIM), lambda b, h, i: (b, i, h)),
        out_shape=jax.ShapeDtypeStruct((B, S, H * V_HEAD_DIM), BF16),
        compiler_params=_params(("arbitrary", "arbitrary", "arbitrary")),
    )(q, k, v)


def _dil_attn_kernel(q_ref, kc_ref, kp_ref, vc_ref, vp_ref, bias_ref, o_ref, lse_ref, *, blk, scale):
    n = pl.program_id(2)
    col = lax.broadcasted_iota(jnp.int32, (blk, 2 * blk), 1)
    prev_ok = jnp.logical_or(col >= blk, n > 0)
    lses = []
    for hd in range(DIL_HEADS):
        sl = slice(hd * DIL_HEAD_DIM, (hd + 1) * DIL_HEAD_DIM)
        q = q_ref[0, :, sl]
        k = jnp.concatenate([kp_ref[0, :, sl], kc_ref[0, :, sl]], axis=0)
        v = jnp.concatenate([vp_ref[0, :, sl], vc_ref[0, :, sl]], axis=0)
        s = lax.dot_general(q, k, _NT, preferred_element_type=F32) * scale + bias_ref[hd]
        s = jnp.where(prev_ok, s, MASK_VALUE)
        m = jnp.max(s, axis=-1, keepdims=True)
        p = jnp.exp(s - m)
        den = jnp.sum(p, axis=-1, keepdims=True)
        o = jnp.dot(p.astype(BF16), v, preferred_element_type=F32) / den
        o_ref[0, :, sl] = o.astype(BF16)
        lses.append(m + jnp.log(den))
    lse_ref[0, 0] = jnp.concatenate(lses, axis=1)


def _dil_attn(proj1, bias_tile, group, dil, batch, seq):
    blk = 128
    sub = seq // dil
    cols_per_row = W1_COLS // W1_TILE
    view = proj1.reshape(batch, sub, dil * W1_COLS)
    qc = GATE_COLS // W1_TILE + group
    kc = qc + DIL_GROUPS
    vc = kc + DIL_GROUPS
    width = DIL_HEADS * DIL_HEAD_DIM

    def cur(c):
        return pl.BlockSpec((1, blk, width), lambda b, r, n: (b, n, r * cols_per_row + c))

    def prev(c):
        return pl.BlockSpec((1, blk, width), lambda b, r, n: (b, jnp.maximum(n - 1, 0), r * cols_per_row + c))

    o, lse = pl.pallas_call(
        functools.partial(_dil_attn_kernel, blk=blk, scale=DIL_HEAD_DIM ** -0.5),
        grid=(batch, dil, sub // blk),
        in_specs=[cur(qc), cur(kc), prev(kc), cur(vc), prev(vc),
                  pl.BlockSpec((DIL_HEADS, blk, 2 * blk), lambda b, r, n: (0, 0, 0))],
        out_specs=[pl.BlockSpec((1, blk, width), lambda b, r, n: (b, n, r)),
                   pl.BlockSpec((1, 1, blk, DIL_HEADS), lambda b, r, n: (b, r, n, 0))],
        out_shape=[jax.ShapeDtypeStruct((batch, sub, dil * width), BF16),
                   jax.ShapeDtypeStruct((batch, dil, sub, DIL_HEADS), F32)],
        compiler_params=_params(("arbitrary", "arbitrary", "arbitrary")),
    )(view, view, view, view, view, bias_tile)
    o = o.reshape(batch * seq, width)
    lse = lse.transpose(0, 2, 1, 3).reshape(batch * seq, DIL_HEADS)
    return o, lse


def _t5_bucket(n):
    exact = REL_BUCKETS // 2
    nf = jnp.maximum(n, 1).astype(F32)
    large = exact + (jnp.log(nf / exact) / math.log(REL_MAX_DISTANCE / exact)
                     * (REL_BUCKETS - exact)).astype(jnp.int32)
    large = jnp.minimum(large, REL_BUCKETS - 1)
    return jnp.where(n < exact, n, large)


def _dil_bias_tile(rel_bias, group, dil, blk=128):
    offs = dil * jnp.arange(DIL_KEYS, dtype=jnp.int32)
    per_key = rel_bias[_t5_bucket(offs)][:, group * DIL_HEADS:(group + 1) * DIL_HEADS].astype(F32)
    dist = (np.arange(blk)[:, None] + blk) - np.arange(2 * blk)[None, :]
    valid = (dist >= 0) & (dist < DIL_KEYS)
    tile = per_key[np.clip(dist, 0, DIL_KEYS - 1)]
    tile = jnp.where(valid[:, :, None], tile, MASK_VALUE)
    return tile.transpose(2, 0, 1)


def _outproj_kernel(x_ref, oa_ref, ob0_ref, ob1_ref, ob2_ref, lse_ref, ga_ref, gb_ref, mods_ref,
                    woa_ref, wob_ref, wout_ref, g2_ref, wr_ref, br_ref,
                    x1_ref, h2_ref, logit_ref):
    lse = lse_ref[...]
    ls = [lse[:, g * DIL_HEADS:(g + 1) * DIL_HEADS] for g in range(DIL_GROUPS)]
    m = functools.reduce(jnp.maximum, ls)
    es = [jnp.exp(l - m) for l in ls]
    tot = functools.reduce(lambda a, b: a + b, es)
    ws = [e / tot for e in es]
    obs = (ob0_ref, ob1_ref, ob2_ref)
    pieces = []
    for hd in range(DIL_HEADS):
        sl = slice(hd * DIL_HEAD_DIM, (hd + 1) * DIL_HEAD_DIM)
        acc = None
        for g in range(DIL_GROUPS):
            term = ws[g][:, hd:hd + 1] * obs[g][:, sl].astype(F32)
            acc = term if acc is None else acc + term
        pieces.append(acc.astype(BF16))
    ob = jnp.concatenate(pieces, axis=1)
    o_a = jnp.dot(oa_ref[...], woa_ref[...], preferred_element_type=F32)
    o_b = jnp.dot(ob, wob_ref[...], preferred_element_type=F32)
    merged = ga_ref[...].astype(F32) * o_a + gb_ref[...].astype(F32) * o_b
    mix = jnp.dot(merged.astype(BF16), wout_ref[...], preferred_element_type=F32)
    x1 = x_ref[...] + mods_ref[0, GT1:GT1 + 1, :] * mix
    x1_ref[...] = x1
    h2 = _norm_modulate(x1, g2_ref[...], mods_ref[0, SC2:SC2 + 1, :], mods_ref[0, SH2:SH2 + 1, :])
    h2_ref[...] = h2.astype(BF16)
    logit_ref[...] = jnp.dot(h2, wr_ref[...], preferred_element_type=F32,
                             precision=lax.Precision.HIGHEST) + br_ref[...]


def _outproj(x2d, o_mla, o_dil, lse_all, proj1, mods, w_oa, w_ob, w_o, norm2_g, w_router, b_router, seq):
    T = x2d.shape[0]
    tm = 256
    tiles_per_batch = seq // tm
    row = lambda i: (i, 0)
    const = lambda i: (0, 0)
    one_buf = pl.Buffered(1)
    width = DIL_HEADS * DIL_HEAD_DIM
    return pl.pallas_call(
        _outproj_kernel,
        grid=(T // tm,),
        in_specs=[pl.BlockSpec((tm, D_MODEL), row),
                  pl.BlockSpec((tm, width), row),
                  pl.BlockSpec((tm, width), row),
                  pl.BlockSpec((tm, width), row),
                  pl.BlockSpec((tm, width), row),
                  pl.BlockSpec((tm, DIL_GROUPS * DIL_HEADS), row),
                  pl.BlockSpec((tm, D_MODEL), lambda i: (i, 0)),
                  pl.BlockSpec((tm, D_MODEL), lambda i: (i, 1)),
                  pl.BlockSpec((1, N_MODS, D_MODEL), lambda i: (i // tiles_per_batch, 0, 0)),
                  pl.BlockSpec(w_oa.shape, const, pipeline_mode=one_buf),
                  pl.BlockSpec(w_ob.shape, const, pipeline_mode=one_buf),
                  pl.BlockSpec(w_o.shape, const, pipeline_mode=one_buf),
                  pl.BlockSpec((1, D_MODEL), const),
                  pl.BlockSpec(w_router.shape, const, pipeline_mode=one_buf),
                  pl.BlockSpec((1, LANES), const)],
        out_specs=[pl.BlockSpec((tm, D_MODEL), row),
                   pl.BlockSpec((tm, D_MODEL), row),
                   pl.BlockSpec((tm, LANES), row)],
        out_shape=[jax.ShapeDtypeStruct((T, D_MODEL), F32),
                   jax.ShapeDtypeStruct((T, D_MODEL), BF16),
                   jax.ShapeDtypeStruct((T, LANES), F32)],
        compiler_params=_params(("arbitrary",)),
    )(x2d, o_mla, o_dil[0], o_dil[1], o_dil[2], lse_all, proj1, proj1, mods,
      w_oa, w_ob, w_o, norm2_g.reshape(1, D_MODEL), w_router, b_router)


def _moe_kernel(be_ref, nused_ref, x_ref, wg_ref, wu_ref, wd_ref, o_ref):
    i = pl.program_id(0)

    @pl.when(i < nused_ref[0])
    def _():
        x = x_ref[...]
        g = jnp.dot(x, wg_ref[0], preferred_element_type=F32)
        u = jnp.dot(x, wu_ref[0], preferred_element_type=F32)
        a = (g * jax.nn.sigmoid(g) * u).astype(BF16)
        o_ref[...] = jnp.dot(a, wd_ref[0], preferred_element_type=F32).astype(BF16)

    @pl.when(i >= nused_ref[0])
    def _():
        o_ref[...] = jnp.zeros_like(o_ref)


def _moe_ffn(x_buf, block_exp, n_used, w_gate, w_up, w_down, blk):
    P = x_buf.shape[0]
    wsel = lambda i, be, nu: (be[i], 0, 0)
    row = lambda i, be, nu: (i, 0)
    return pl.pallas_call(
        _moe_kernel,
        grid_spec=pltpu.PrefetchScalarGridSpec(
            num_scalar_prefetch=2,
            grid=(P // blk,),
            in_specs=[pl.BlockSpec((blk, D_MODEL), row),
                      pl.BlockSpec((1, D_MODEL, EXPERT_FF), wsel),
                      pl.BlockSpec((1, D_MODEL, EXPERT_FF), wsel),
                      pl.BlockSpec((1, EXPERT_FF, D_MODEL), wsel)],
            out_specs=pl.BlockSpec((blk, D_MODEL), row)),
        out_shape=jax.ShapeDtypeStruct((P, D_MODEL), BF16),
        compiler_params=_params(("arbitrary",)),
    )(block_exp, n_used, x_buf, w_gate, w_up, w_down)


def _final_kernel(x1_ref, y0_ref, y1_ref, w_ref, mods_ref, g_ref, o_ref):
    w = w_ref[...]
    y = w[:, 0:1] * y0_ref[...].astype(F32) + w[:, 1:2] * y1_ref[...].astype(F32)
    x2 = x1_ref[...] + mods_ref[0, GT2:GT2 + 1, :] * y
    o_ref[...] = _rms(x2, g_ref[...])


def _final(x1, y0, y1, weights, mods, final_g, seq):
    T = x1.shape[0]
    tm = 512
    tiles_per_batch = seq // tm
    row = lambda i: (i, 0)
    return pl.pallas_call(
        _final_kernel,
        grid=(T // tm,),
        in_specs=[pl.BlockSpec((tm, D_MODEL), row),
                  pl.BlockSpec((tm, D_MODEL), row),
                  pl.BlockSpec((tm, D_MODEL), row),
                  pl.BlockSpec((tm, TOP_K), row),
                  pl.BlockSpec((1, N_MODS, D_MODEL), lambda i: (i // tiles_per_batch, 0, 0)),
                  pl.BlockSpec((1, D_MODEL), lambda i: (0, 0))],
        out_specs=pl.BlockSpec((tm, D_MODEL), row),
        out_shape=jax.ShapeDtypeStruct((T, D_MODEL), F32),
        compiler_params=_params(("arbitrary",)),
    )(x1, y0, y1, weights, mods, final_g.reshape(1, D_MODEL))


def _route(logits, blk):
    T = logits.shape[0]
    g_logits = logits[:, :N_GROUPS]
    e_logits = logits[:, N_GROUPS:N_GROUPS + N_EXPERTS].reshape(T, N_GROUPS, EXPERTS_PER_GROUP)
    _, g_idx = lax.top_k(g_logits, 1)
    p_group = jnp.take_along_axis(jax.nn.softmax(g_logits, axis=-1), g_idx, axis=1)[:, 0]
    e_in = jnp.take_along_axis(e_logits, g_idx[:, :, None], axis=1)[:, 0]
    top_v, top_i = lax.top_k(e_in, TOP_K)
    expert_id = g_idx * EXPERTS_PER_GROUP + top_i
    weights = p_group[:, None] * jax.nn.softmax(top_v, axis=-1)
    A = T * TOP_K
    e_flat = expert_id.reshape(A)
    t_flat = jnp.repeat(jnp.arange(T, dtype=jnp.int32), TOP_K)
    order = jnp.argsort(e_flat)
    e_s, t_s = e_flat[order], t_flat[order]
    counts = jax.ops.segment_sum(jnp.ones((A,), jnp.int32), e_flat, num_segments=N_EXPERTS)
    starts = jnp.cumsum(counts) - counts
    padded = (counts + blk - 1) // blk * blk
    pends = jnp.cumsum(padded)
    pstarts = pends - padded
    dest = pstarts[e_s] + (jnp.arange(A, dtype=jnp.int32) - starts[e_s])
    n_blocks = -(-A // blk) + N_EXPERTS
    buf_tok = jnp.zeros((n_blocks * blk,), jnp.int32).at[dest].set(t_s)
    slot = jnp.zeros((A,), jnp.int32).at[order].set(dest).reshape(T, TOP_K)
    block_exp = jnp.clip(jnp.searchsorted(pends, jnp.arange(n_blocks, dtype=jnp.int32) * blk,
                                          side='right'), 0, N_EXPERTS - 1).astype(jnp.int32)
    n_used = (pends[-1] // blk).astype(jnp.int32).reshape(1)
    return weights, buf_tok, slot, block_exp, n_used


def _prep_weights(w_in, w_uq, w_ukv):
    q0 = Q_LORA_RANK + KV_LORA_RANK
    d0 = q0 + QK_ROPE_DIM
    half = QK_ROPE_DIM // 2
    pad = jnp.zeros((D_MODEL, LANES - QK_ROPE_DIM), F32)
    k_r = w_in[:, q0:d0]
    k_r_sw = jnp.concatenate([k_r[:, half:], k_r[:, :half]], axis=1)
    w_mla = jnp.concatenate([w_in[:, :q0], k_r, pad, k_r_sw, pad], axis=1).astype(BF16)
    dil = w_in[:, d0:d0 + 3 * DIL_COLS]
    gates = w_in[:, d0 + 3 * DIL_COLS:]
    w1 = jnp.concatenate([gates, dil], axis=1).astype(BF16)
    uq = w_uq.reshape(Q_LORA_RANK, MLA_HEADS, QK_NOPE_DIM + QK_ROPE_DIM)
    nope = uq[:, :, :QK_NOPE_DIM].reshape(Q_LORA_RANK, -1)
    rope = uq[:, :, QK_NOPE_DIM:]
    rope_sw = jnp.concatenate([rope[:, :, half:], rope[:, :, :half]], axis=2)
    zpad = jnp.zeros((Q_LORA_RANK, MLA_HEADS, LANES - QK_ROPE_DIM), F32)
    rope = jnp.concatenate([rope, zpad], axis=2).reshape(Q_LORA_RANK, -1)
    rope_sw = jnp.concatenate([rope_sw, zpad], axis=2).reshape(Q_LORA_RANK, -1)
    w_uq_ext = jnp.concatenate([nope, rope, rope_sw], axis=1).astype(BF16)
    ukv = w_ukv.reshape(KV_LORA_RANK, MLA_HEADS, QK_NOPE_DIM + V_HEAD_DIM)
    w_ukv_r = jnp.concatenate([ukv[:, :, :QK_NOPE_DIM].reshape(KV_LORA_RANK, -1),
                               ukv[:, :, QK_NOPE_DIM:].reshape(KV_LORA_RANK, -1)], axis=1).astype(BF16)
    return w_mla, w1, w_uq_ext, w_ukv_r


def _rope_tables(positions):
    half = QK_ROPE_DIM // 2
    inv_freq = 1.0 / (ROPE_THETA ** (jnp.arange(half, dtype=F32) / half))
    ang = positions.astype(F32)[..., None] * inv_freq
    cos, sin = jnp.cos(ang), jnp.sin(ang)
    zero = jnp.zeros(cos.shape[:-1] + (LANES - QK_ROPE_DIM,), F32)
    return (jnp.concatenate([cos, cos, zero], axis=-1),
            jnp.concatenate([-sin, sin, zero], axis=-1))


def kernel(x, c, positions, mod_w, mod_b, norm1_g, w_in, q_norm_g, w_uq, kv_norm_g, w_ukv, rel_bias,
           w_o_mla, w_o_dil, w_out, norm2_g, w_group, b_group, w_expert, b_expert, w_gate, w_up,
           w_down, final_g):
    B, S, D = x.shape
    T = B * S
    layer = 0
    x2d = x.reshape(T, D)
    mods = _mods(c, mod_w[layer], mod_b[layer])
    w_mla, w1, w_uq_ext, w_ukv_r = _prep_weights(w_in[layer], w_uq[layer], w_ukv[layer])
    cos_t, sin_t = _rope_tables(positions)

    proj1 = _inproj(x2d, mods, norm1_g[layer], w1, S)
    q, k, v = _mla_prep(x, mods, norm1_g[layer], w_mla, q_norm_g[layer], w_uq_ext,
                        kv_norm_g[layer], w_ukv_r, cos_t, sin_t)
    o_mla = _mla_attn(q, k, v).reshape(T, MLA_HEADS * V_HEAD_DIM)

    proj1_3d = proj1.reshape(B, S, W1_COLS)
    o_dil, lses = [], []
    for g, (_, dil) in enumerate(DIL_PAIRS):
        o_g, lse_g = _dil_attn(proj1_3d, _dil_bias_tile(rel_bias, g, dil), g, dil, B, S)
        o_dil.append(o_g)
        lses.append(lse_g)
    lse_all = jnp.concatenate(lses, axis=1)

    w_router = jnp.zeros((D, LANES), F32).at[:, :N_GROUPS].set(w_group[layer]) \
        .at[:, N_GROUPS:N_GROUPS + N_EXPERTS].set(w_expert[layer])
    b_router = jnp.zeros((1, LANES), F32).at[0, :N_GROUPS].set(b_group[layer]) \
        .at[0, N_GROUPS:N_GROUPS + N_EXPERTS].set(b_expert[layer])
    x1, h2, logits = _outproj(x2d, o_mla, o_dil, lse_all, proj1, mods,
                              w_o_mla[layer].astype(BF16), w_o_dil[layer].astype(BF16),
                              w_out[layer].astype(BF16), norm2_g[layer], w_router, b_router, S)

    moe_blk = 256
    weights, buf_tok, slot, block_exp, n_used = _route(logits, moe_blk)
    x_buf = h2[buf_tok]
    y_buf = _moe_ffn(x_buf, block_exp, n_used, w_gate[layer].astype(BF16), w_up[layer].astype(BF16),
                     w_down[layer].astype(BF16), moe_blk)
    out = _final(x1, y_buf[slot[:, 0]], y_buf[slot[:, 1]], weights, mods, final_g, S)
    return out.reshape(B, S, D)
```

```python
import functools
import math

import jax
import jax.numpy as jnp
import numpy as np
from jax import lax
from jax.experimental import pallas as pl
from jax.experimental.pallas import tpu as pltpu

F32 = jnp.float32
BF16 = jnp.bfloat16

D_MODEL = 2048
MLA_HEADS = 8
Q_LORA_RANK = 512
KV_LORA_RANK = 512
QK_NOPE_DIM = 128
QK_ROPE_DIM = 64
V_HEAD_DIM = 128
ROPE_THETA = 10000.0
DIL_PAIRS = ((128, 1), (512, 4), (2048, 16))
DIL_GROUPS = len(DIL_PAIRS)
DIL_HEADS = 8
DIL_HEAD_DIM = 128
DIL_WIDTH = DIL_HEADS * DIL_HEAD_DIM
DIL_KEYS = 129
REL_BUCKETS = 32
REL_MAX_DISTANCE = 2048
N_GROUPS = 8
EXPERTS_PER_GROUP = 8
N_EXPERTS = N_GROUPS * EXPERTS_PER_GROUP
TOP_K = 2
EXPERT_FF = 1408
N_MODS = 6
EPS = 1e-6

LANES = 128
MLA_QK_PAD = 256
GATE_COLS = 2 * D_MODEL
COL_TILE = 1024
LSE_LANES = LANES // DIL_HEADS
ROUTE_E0 = N_GROUPS
MASK_VALUE = -1e30
VMEM_LIMIT = 56 * 1024 * 1024
LOG2E = math.log2(math.e)

SH1, SC1, GT1, SH2, SC2, GT2 = range(N_MODS)


def _params(sem, vmem=VMEM_LIMIT):
    return pltpu.CompilerParams(dimension_semantics=sem, vmem_limit_bytes=vmem)


def _norm_modulate(x, g, scale, shift):
    ms = jnp.mean(x * x, axis=-1, keepdims=True)
    return x * lax.rsqrt(ms + EPS) * g * (1.0 + scale) + shift


def _rms(x, g):
    return x * lax.rsqrt(jnp.mean(x * x, axis=-1, keepdims=True) + EPS) * g


def _mods_kernel(c_ref, w_ref, b_ref, o_ref):
    c = c_ref[...]
    act = c * jax.nn.sigmoid(c)
    o_ref[...] = jnp.dot(act, w_ref[...], preferred_element_type=F32,
                         precision=lax.Precision.HIGHEST) + b_ref[...]


def _mods(c, mod_w, mod_b):
    B = c.shape[0]
    rows = 8
    c_pad = jnp.zeros((rows, D_MODEL), F32).at[:B].set(c)
    n = mod_w.shape[1]
    tn = 1536
    out = pl.pallas_call(
        _mods_kernel,
        grid=(n // tn,),
        in_specs=[pl.BlockSpec((rows, D_MODEL), lambda j: (0, 0)),
                  pl.BlockSpec((D_MODEL, tn), lambda j: (0, j)),
                  pl.BlockSpec((1, tn), lambda j: (0, j))],
        out_specs=pl.BlockSpec((rows, tn), lambda j: (0, j)),
        out_shape=jax.ShapeDtypeStruct((rows, n), F32),
        compiler_params=_params(("arbitrary",)),
        name="mods",
    )(c_pad, mod_w, mod_b.reshape(1, n))
    return out[:B].reshape(B, N_MODS, D_MODEL)


def _inproj_gate_kernel(x_ref, mods_ref, g_ref, w_ref, o_ref, h_ref, *, gate_tiles):
    j = pl.program_id(1)

    @pl.when(j == 0)
    def _():
        h = _norm_modulate(x_ref[...], g_ref[...], mods_ref[0, SC1:SC1 + 1, :],
                           mods_ref[0, SH1:SH1 + 1, :])
        h_ref[...] = h.astype(BF16)

    acc = jnp.dot(h_ref[...], w_ref[...], preferred_element_type=F32)

    @pl.when(j < gate_tiles)
    def _():
        o_ref[...] = jax.nn.sigmoid(acc).astype(BF16)

    @pl.when(j >= gate_tiles)
    def _():
        o_ref[...] = acc.astype(BF16)


def _inproj_gate(x2d, mods, norm_g, w, seq):
    T = x2d.shape[0]
    tm = 1024
    tiles_per_batch = seq // tm
    ncols = w.shape[1]
    return pl.pallas_call(
        functools.partial(_inproj_gate_kernel, gate_tiles=GATE_COLS // COL_TILE),
        grid=(T // tm, ncols // COL_TILE),
        in_specs=[pl.BlockSpec((tm, D_MODEL), lambda i, j: (i, 0)),
                  pl.BlockSpec((1, N_MODS, D_MODEL), lambda i, j: (i // tiles_per_batch, 0, 0)),
                  pl.BlockSpec((1, D_MODEL), lambda i, j: (0, 0)),
                  pl.BlockSpec((D_MODEL, COL_TILE), lambda i, j: (0, j))],
        out_specs=[pl.BlockSpec((tm, COL_TILE), lambda i, j: (i, j)),
                   pl.BlockSpec((tm, D_MODEL), lambda i, j: (i, 0))],
        out_shape=[jax.ShapeDtypeStruct((T, ncols), BF16),
                   jax.ShapeDtypeStruct((T, D_MODEL), BF16)],
        compiler_params=_params(("arbitrary", "arbitrary")),
        name="inproj_gate",
    )(x2d, mods, norm_g.reshape(1, D_MODEL), w)


def _inproj_dil_kernel(h_ref, w_ref, o_ref, scr, *, dil):
    acc = jnp.dot(h_ref[...], w_ref[...], preferred_element_type=F32)
    tm = acc.shape[0]
    n = tm // dil
    chunks = acc.shape[1] // LANES
    for c in range(chunks):
        scr[c] = acc[:, c * LANES:(c + 1) * LANES]
    for r in range(dil):
        for c in range(chunks):
            o_ref[0, r, :, c * LANES:(c + 1) * LANES] = scr[c, pl.ds(r, n, stride=dil), :].astype(BF16)


def _inproj_dil(h2d, w, dil, batch, seq):
    tm = 1024
    tiles_per_batch = seq // tm
    ncols = w.shape[1]
    return pl.pallas_call(
        functools.partial(_inproj_dil_kernel, dil=dil),
        grid=(batch * tiles_per_batch, ncols // COL_TILE),
        in_specs=[pl.BlockSpec((tm, D_MODEL), lambda i, j: (i, 0)),
                  pl.BlockSpec((D_MODEL, COL_TILE), lambda i, j: (0, j))],
        out_specs=pl.BlockSpec((1, dil, tm // dil, COL_TILE),
                               lambda i, j: (i // tiles_per_batch, 0, i % tiles_per_batch, j)),
        out_shape=jax.ShapeDtypeStruct((batch, dil, seq // dil, ncols), BF16),
        scratch_shapes=[pltpu.VMEM((COL_TILE // LANES, tm, LANES), F32)],
        compiler_params=_params(("arbitrary", "arbitrary")),
        name=f"inproj_dil{dil}",
    )(h2d, w)


def _mla_prep_kernel(h_ref, wm_ref, qg_ref, wuq_ref, kvg_ref, wukv_ref,
                     cos_ref, sin_ref, qt_ref, k_ref, vt_ref, *, scale):
    cm = jnp.dot(h_ref[0], wm_ref[...], preferred_element_type=F32)
    cos = cos_ref[0]
    sin = sin_ref[0]
    cq = _rms(cm[:, :Q_LORA_RANK], qg_ref[...]).astype(BF16)
    qx = jnp.dot(cq, wuq_ref[...], preferred_element_type=F32)
    ckv = _rms(cm[:, Q_LORA_RANK:Q_LORA_RANK + KV_LORA_RANK], kvg_ref[...]).astype(BF16)
    kvx = jnp.dot(ckv, wukv_ref[...], preferred_element_type=F32)
    kr0 = Q_LORA_RANK + KV_LORA_RANK
    k_rope = (cm[:, kr0:kr0 + LANES] * cos + cm[:, kr0 + LANES:kr0 + 2 * LANES] * sin).astype(BF16)
    hn = MLA_HEADS * QK_NOPE_DIM
    for hd in range(MLA_HEADS):
        lo = hd * LANES
        q_rope = qx[:, hn + lo:hn + lo + LANES] * cos + qx[:, 2 * hn + lo:2 * hn + lo + LANES] * sin
        qt_ref[0, hd, :LANES, :] = (qx[:, lo:lo + LANES] * scale).T.astype(BF16)
        qt_ref[0, hd, LANES:, :] = (q_rope * scale).T.astype(BF16)
        k_ref[0, hd, :, :LANES] = kvx[:, lo:lo + LANES].astype(BF16)
        k_ref[0, hd, :, LANES:] = k_rope
        vt_ref[0, hd] = kvx[:, hn + lo:hn + lo + LANES].T.astype(BF16)


def _mla_prep(h3d, w_mla, q_norm_g, w_uq_ext, kv_norm_g, w_ukv_r, cos_t, sin_t):
    B, S, _ = h3d.shape
    tm = 512
    const2 = lambda b, i: (0, 0)
    scale = (QK_NOPE_DIM + QK_ROPE_DIM) ** -0.5 * LOG2E
    return pl.pallas_call(
        functools.partial(_mla_prep_kernel, scale=scale),
        grid=(B, S // tm),
        in_specs=[pl.BlockSpec((1, tm, D_MODEL), lambda b, i: (b, i, 0)),
                  pl.BlockSpec(w_mla.shape, const2),
                  pl.BlockSpec((1, Q_LORA_RANK), const2),
                  pl.BlockSpec(w_uq_ext.shape, const2),
                  pl.BlockSpec((1, KV_LORA_RANK), const2),
                  pl.BlockSpec(w_ukv_r.shape, const2),
                  pl.BlockSpec((1, tm, LANES), lambda b, i: (b, i, 0)),
                  pl.BlockSpec((1, tm, LANES), lambda b, i: (b, i, 0))],
        out_specs=[pl.BlockSpec((1, MLA_HEADS, MLA_QK_PAD, tm), lambda b, i: (b, 0, 0, i)),
                   pl.BlockSpec((1, MLA_HEADS, tm, MLA_QK_PAD), lambda b, i: (b, 0, i, 0)),
                   pl.BlockSpec((1, MLA_HEADS, V_HEAD_DIM, tm), lambda b, i: (b, 0, 0, i))],
        out_shape=[jax.ShapeDtypeStruct((B, MLA_HEADS, MLA_QK_PAD, S), BF16),
                   jax.ShapeDtypeStruct((B, MLA_HEADS, S, MLA_QK_PAD), BF16),
                   jax.ShapeDtypeStruct((B, MLA_HEADS, V_HEAD_DIM, S), BF16)],
        compiler_params=_params(("arbitrary", "arbitrary")),
        name="mla_prep",
    )(h3d, w_mla, q_norm_g.reshape(1, -1), w_uq_ext, kv_norm_g.reshape(1, -1), w_ukv_r, cos_t, sin_t)


def _mla_attn_kernel(qt_ref, k_ref, vt_ref, o_ref, *, tq, tk):
    i = pl.program_id(2)
    qt = qt_ref[0, 0]
    subs = tq // tk

    def step(j, carry, masked):
        m, l, acc = carry
        for s_i in range(subs):
            start = pl.multiple_of(j * tq + s_i * tk, tk)
            k = k_ref[0, 0, pl.ds(start, tk), :]
            vt = vt_ref[0, 0, :, pl.ds(start, tk)]
            st = jnp.dot(k, qt, preferred_element_type=F32)
            if masked:
                krow = lax.broadcasted_iota(jnp.int32, (tk, tq), 0) + s_i * tk
                qcol = lax.broadcasted_iota(jnp.int32, (tk, tq), 1)
                st = jnp.where(krow <= qcol, st, MASK_VALUE)
            m_new = jnp.maximum(m, jnp.max(st, axis=0, keepdims=True))
            alpha = jnp.exp2(m - m_new)
            p = jnp.exp2(st - m_new)
            l = alpha * l + jnp.sum(p, axis=0, keepdims=True)
            acc = alpha * acc + jnp.dot(vt, p.astype(BF16), preferred_element_type=F32)
            m = m_new
        return m, l, acc

    init = (jnp.full((1, tq), MASK_VALUE, F32), jnp.zeros((1, tq), F32),
            jnp.zeros((V_HEAD_DIM, tq), F32))
    carry = lax.fori_loop(0, i, functools.partial(step, masked=False), init)
    _, l, acc = step(i, carry, True)
    o_ref[0] = (acc / l).T.astype(BF16)


def _mla_attn(qt, k, vt):
    B, H, _, S = qt.shape
    tq, tk = 1024, 512
    return pl.pallas_call(
        functools.partial(_mla_attn_kernel, tq=tq, tk=tk),
        grid=(B, H, S // tq),
        in_specs=[pl.BlockSpec((1, 1, MLA_QK_PAD, tq), lambda b, h, i: (b, h, 0, i)),
                  pl.BlockSpec((1, 1, S, MLA_QK_PAD), lambda b, h, i: (b, h, 0, 0)),
                  pl.BlockSpec((1, 1, V_HEAD_DIM, S), lambda b, h, i: (b, h, 0, 0))],
        out_specs=pl.BlockSpec((1, tq, V_HEAD_DIM), lambda b, h, i: (b, i, h)),
        out_shape=jax.ShapeDtypeStruct((B, S, H * V_HEAD_DIM), BF16),
        compiler_params=_params(("arbitrary", "arbitrary", "arbitrary")),
        name="mla_attn",
    )(qt, k, vt)


_NT = (((1,), (1,)), ((), ()))


def _dil_attn_kernel(q_ref, kc_ref, kp_ref, vc_ref, vp_ref, bias_ref, o_ref, lse_ref, *, blk, scale):
    n = pl.program_id(2)
    col = lax.broadcasted_iota(jnp.int32, (blk, 2 * blk), 1)
    prev_ok = jnp.logical_or(col >= blk, n > 0)
    head_of_lane = lax.broadcasted_iota(jnp.int32, (blk, LANES), 1) // LSE_LANES
    lse_tile = jnp.zeros((blk, LANES), F32)
    for hd in range(DIL_HEADS):
        sl = slice(hd * DIL_HEAD_DIM, (hd + 1) * DIL_HEAD_DIM)
        q = q_ref[0, 0, :, sl]
        k = jnp.concatenate([kp_ref[0, 0, :, sl], kc_ref[0, 0, :, sl]], axis=0)
        v = jnp.concatenate([vp_ref[0, 0, :, sl], vc_ref[0, 0, :, sl]], axis=0)
        s = lax.dot_general(q, k, _NT, preferred_element_type=F32) * scale + bias_ref[hd]
        s = jnp.where(prev_ok, s, MASK_VALUE)
        m = jnp.max(s, axis=-1, keepdims=True)
        p = jnp.exp(s - m)
        den = jnp.sum(p, axis=-1, keepdims=True)
        o = jnp.dot(p.astype(BF16), v, preferred_element_type=F32) / den
        o_ref[0, 0, :, sl] = o.astype(BF16)
        lse_tile = jnp.where(head_of_lane == hd, m + jnp.log(den), lse_tile)
    lse_ref[0, 0] = lse_tile


def _dil_attn(qkv, bias_tile, dil, qcol):
    batch, _, sub, _ = qkv.shape
    blk = 128

    def cur(c):
        return pl.BlockSpec((1, 1, blk, DIL_WIDTH), lambda b, r, n: (b, r, n, c))

    def prev(c):
        return pl.BlockSpec((1, 1, blk, DIL_WIDTH), lambda b, r, n: (b, r, jnp.maximum(n - 1, 0), c))

    return pl.pallas_call(
        functools.partial(_dil_attn_kernel, blk=blk, scale=DIL_HEAD_DIM ** -0.5),
        grid=(batch, dil, sub // blk),
        in_specs=[cur(qcol), cur(qcol + 1), prev(qcol + 1), cur(qcol + 2), prev(qcol + 2),
                  pl.BlockSpec((DIL_HEADS, blk, 2 * blk), lambda b, r, n: (0, 0, 0))],
        out_specs=[pl.BlockSpec((1, 1, blk, DIL_WIDTH), lambda b, r, n: (b, r, n, 0)),
                   pl.BlockSpec((1, 1, blk, LANES), lambda b, r, n: (b, r, n, 0))],
        out_shape=[jax.ShapeDtypeStruct((batch, dil, sub, DIL_WIDTH), BF16),
                   jax.ShapeDtypeStruct((batch, dil, sub, LANES), F32)],
        compiler_params=_params(("arbitrary", "arbitrary", "arbitrary")),
        name=f"dil_attn{dil}",
    )(qkv, qkv, qkv, qkv, qkv, bias_tile)


def _t5_bucket(n):
    exact = REL_BUCKETS // 2
    nf = jnp.maximum(n, 1).astype(F32)
    large = exact + (jnp.log(nf / exact) / math.log(REL_MAX_DISTANCE / exact)
                     * (REL_BUCKETS - exact)).astype(jnp.int32)
    large = jnp.minimum(large, REL_BUCKETS - 1)
    return jnp.where(n < exact, n, large)


def _dil_bias_tile(rel_bias, group, dil, blk=128):
    offs = dil * jnp.arange(DIL_KEYS, dtype=jnp.int32)
    per_key = rel_bias[_t5_bucket(offs)][:, group * DIL_HEADS:(group + 1) * DIL_HEADS].astype(F32)
    dist = (np.arange(blk)[:, None] + blk) - np.arange(2 * blk)[None, :]
    valid = (dist >= 0) & (dist < DIL_KEYS)
    tile = per_key[np.clip(dist, 0, DIL_KEYS - 1)]
    tile = jnp.where(valid[:, :, None], tile, MASK_VALUE)
    return tile.transpose(2, 0, 1)


def _route_topk(logits):
    lane = lax.broadcasted_iota(jnp.int32, logits.shape, 1)
    neg = -jnp.inf

    def first_argmax(vals):
        best = jnp.max(vals, axis=-1, keepdims=True)
        idx = jnp.min(jnp.where(vals == best, lane, LANES), axis=-1, keepdims=True)
        return best, idx

    is_group = lane < N_GROUPS
    g_best, g_idx = first_argmax(jnp.where(is_group, logits, neg))
    p_group = 1.0 / jnp.sum(jnp.where(is_group, jnp.exp(logits - g_best), 0.0), axis=-1, keepdims=True)
    e_lo = ROUTE_E0 + g_idx * EXPERTS_PER_GROUP
    in_group = jnp.logical_and(lane >= e_lo, lane < e_lo + EXPERTS_PER_GROUP)
    e_vals = jnp.where(in_group, logits, neg)
    v1, i1 = first_argmax(e_vals)
    v2, i2 = first_argmax(jnp.where(lane == i1, neg, e_vals))
    e21 = jnp.exp(v2 - v1)
    w1 = p_group / (1.0 + e21)
    w2 = p_group * e21 / (1.0 + e21)
    ids = jnp.where(lane == 0, i1 - ROUTE_E0, jnp.where(lane == 1, i2 - ROUTE_E0, 0))
    wts = jnp.where(lane == 0, w1, jnp.where(lane == 1, w2, 0.0))
    return ids, wts


def _outproj_kernel(x_ref, oa_ref, ob0_ref, ob1_ref, ob2_ref, l0_ref, l1_ref, l2_ref, ga_ref, gb_ref,
                    mods_ref, woa_ref, wob_ref, wout_ref, g2_ref, wr_ref, br_ref,
                    x1_ref, h2_ref, ids_ref, wts_ref, og_scr, lse_scr):
    tm = x_ref.shape[0]
    chunks = DIL_WIDTH // LANES
    for gi, (o_ref, l_ref) in enumerate(((ob1_ref, l1_ref), (ob2_ref, l2_ref))):
        dil = DIL_PAIRS[gi + 1][1]
        n = tm // dil
        for r in range(dil):
            lse_scr[gi, pl.ds(r, n, stride=dil), :] = l_ref[0, r]
            for c in range(chunks):
                og_scr[gi, c, pl.ds(r, n, stride=dil), :] = \
                    o_ref[0, r, :, c * LANES:(c + 1) * LANES].astype(F32)
    ls = [l0_ref[0, 0], lse_scr[0], lse_scr[1]]
    m = functools.reduce(jnp.maximum, ls)
    es = [jnp.exp(l - m) for l in ls]
    tot = functools.reduce(lambda a, b: a + b, es)
    ws = [e / tot for e in es]
    pieces = []
    for hd in range(DIL_HEADS):
        sl = slice(hd * DIL_HEAD_DIM, (hd + 1) * DIL_HEAD_DIM)
        wl = hd * LSE_LANES
        acc = ws[0][:, wl:wl + 1] * ob0_ref[0, 0, :, sl].astype(F32)
        acc = acc + ws[1][:, wl:wl + 1] * og_scr[0, hd]
        acc = acc + ws[2][:, wl:wl + 1] * og_scr[1, hd]
        pieces.append(acc.astype(BF16))
    ob = jnp.concatenate(pieces, axis=1)
    o_a = jnp.dot(oa_ref[...], woa_ref[...], preferred_element_type=F32)
    o_b = jnp.dot(ob, wob_ref[...], preferred_element_type=F32)
    merged = ga_ref[...].astype(F32) * o_a + gb_ref[...].astype(F32) * o_b
    mix = jnp.dot(merged.astype(BF16), wout_ref[...], preferred_element_type=F32)
    x1 = x_ref[...] + mods_ref[0, GT1:GT1 + 1, :] * mix
    x1_ref[...] = x1
    h2 = _norm_modulate(x1, g2_ref[...], mods_ref[0, SC2:SC2 + 1, :], mods_ref[0, SH2:SH2 + 1, :])
    h2_ref[...] = h2.astype(BF16)
    logits = jnp.dot(h2, wr_ref[...], preferred_element_type=F32,
                     precision=lax.Precision.HIGHEST) + br_ref[...]
    ids, wts = _route_topk(logits)
    ids_ref[...] = ids
    wts_ref[...] = wts


def _outproj(x2d, o_mla, o_dil, lse_dil, proj_gate, mods, w_oa, w_ob, w_o, norm2_g, w_router, b_router, seq):
    T = x2d.shape[0]
    tm = 256
    tpb = seq // tm
    row = lambda i: (i, 0)
    const = lambda i: (0, 0)
    one_buf = pl.Buffered(1)

    def deint(width, dil):
        return pl.BlockSpec((1, dil, tm // dil, width), lambda i: (i // tpb, 0, i % tpb, 0))

    dils = [d for _, d in DIL_PAIRS]
    return pl.pallas_call(
        _outproj_kernel,
        grid=(T // tm,),
        in_specs=[pl.BlockSpec((tm, D_MODEL), row),
                  pl.BlockSpec((tm, DIL_WIDTH), row),
                  deint(DIL_WIDTH, dils[0]), deint(DIL_WIDTH, dils[1]), deint(DIL_WIDTH, dils[2]),
                  deint(LANES, dils[0]), deint(LANES, dils[1]), deint(LANES, dils[2]),
                  pl.BlockSpec((tm, D_MODEL), lambda i: (i, 0)),
                  pl.BlockSpec((tm, D_MODEL), lambda i: (i, 1)),
                  pl.BlockSpec((1, N_MODS, D_MODEL), lambda i: (i // tpb, 0, 0)),
                  pl.BlockSpec(w_oa.shape, const, pipeline_mode=one_buf),
                  pl.BlockSpec(w_ob.shape, const, pipeline_mode=one_buf),
                  pl.BlockSpec(w_o.shape, const, pipeline_mode=one_buf),
                  pl.BlockSpec((1, D_MODEL), const),
                  pl.BlockSpec(w_router.shape, const, pipeline_mode=one_buf),
                  pl.BlockSpec((1, LANES), const)],
        out_specs=[pl.BlockSpec((tm, D_MODEL), row),
                   pl.BlockSpec((tm, D_MODEL), row),
                   pl.BlockSpec((tm, LANES), row),
                   pl.BlockSpec((tm, LANES), row)],
        out_shape=[jax.ShapeDtypeStruct((T, D_MODEL), F32),
                   jax.ShapeDtypeStruct((T, D_MODEL), BF16),
                   jax.ShapeDtypeStruct((T, LANES), jnp.int32),
                   jax.ShapeDtypeStruct((T, LANES), F32)],
        scratch_shapes=[pltpu.VMEM((DIL_GROUPS - 1, DIL_WIDTH // LANES, tm, LANES), F32),
                        pltpu.VMEM((DIL_GROUPS - 1, tm, LANES), F32)],
        compiler_params=_params(("arbitrary",)),
        name="outproj",
    )(x2d, o_mla, o_dil[0], o_dil[1], o_dil[2], lse_dil[0], lse_dil[1], lse_dil[2],
      proj_gate, proj_gate, mods, w_oa, w_ob, w_o, norm2_g.reshape(1, D_MODEL), w_router, b_router)


def _moe_kernel(be_ref, nused_ref, x_ref, wg_ref, wu_ref, wd_ref, o_ref):
    i = pl.program_id(0)

    @pl.when(i < nused_ref[0])
    def _():
        x = x_ref[...]
        g = jnp.dot(x, wg_ref[0], preferred_element_type=F32)
        u = jnp.dot(x, wu_ref[0], preferred_element_type=F32)
        a = (g * jax.nn.sigmoid(g) * u).astype(BF16)
        o_ref[...] = jnp.dot(a, wd_ref[0], preferred_element_type=F32).astype(BF16)

    @pl.when(i >= nused_ref[0])
    def _():
        o_ref[...] = jnp.zeros_like(o_ref)


def _moe_ffn(x_buf, block_exp, n_used, w_gate, w_up, w_down, blk):
    P = x_buf.shape[0]
    wsel = lambda i, be, nu: (be[i], 0, 0)
    row = lambda i, be, nu: (i, 0)
    return pl.pallas_call(
        _moe_kernel,
        grid_spec=pltpu.PrefetchScalarGridSpec(
            num_scalar_prefetch=2,
            grid=(P // blk,),
            in_specs=[pl.BlockSpec((blk, D_MODEL), row),
                      pl.BlockSpec((1, D_MODEL, EXPERT_FF), wsel),
                      pl.BlockSpec((1, D_MODEL, EXPERT_FF), wsel),
                      pl.BlockSpec((1, EXPERT_FF, D_MODEL), wsel)],
            out_specs=pl.BlockSpec((blk, D_MODEL), row)),
        out_shape=jax.ShapeDtypeStruct((P, D_MODEL), BF16),
        compiler_params=_params(("arbitrary",)),
        name="moe_ffn",
    )(block_exp, n_used, x_buf, w_gate, w_up, w_down)


def _final_kernel(x1_ref, y0_ref, y1_ref, w_ref, mods_ref, g_ref, o_ref):
    w = w_ref[...]
    y = w[:, 0:1] * y0_ref[...].astype(F32) + w[:, 1:2] * y1_ref[...].astype(F32)
    x2 = x1_ref[...] + mods_ref[0, GT2:GT2 + 1, :] * y
    o_ref[...] = _rms(x2, g_ref[...])


def _final(x1, y0, y1, weights, mods, final_g, seq):
    T = x1.shape[0]
    tm = 512
    tiles_per_batch = seq // tm
    row = lambda i: (i, 0)
    return pl.pallas_call(
        _final_kernel,
        grid=(T // tm,),
        in_specs=[pl.BlockSpec((tm, D_MODEL), row),
                  pl.BlockSpec((tm, D_MODEL), row),
                  pl.BlockSpec((tm, D_MODEL), row),
                  pl.BlockSpec((tm, LANES), row),
                  pl.BlockSpec((1, N_MODS, D_MODEL), lambda i: (i // tiles_per_batch, 0, 0)),
                  pl.BlockSpec((1, D_MODEL), lambda i: (0, 0))],
        out_specs=pl.BlockSpec((tm, D_MODEL), row),
        out_shape=jax.ShapeDtypeStruct((T, D_MODEL), F32),
        compiler_params=_params(("arbitrary",)),
        name="final",
    )(x1, y0, y1, weights, mods, final_g.reshape(1, D_MODEL))


def _route_slots(expert_id, blk):
    T = expert_id.shape[0]
    A = T * TOP_K
    experts = jnp.arange(N_EXPERTS, dtype=jnp.int32)
    key = expert_id.reshape(A) * A + jnp.arange(A, dtype=jnp.int32)
    skey = jnp.sort(key)
    e_s = skey // A
    order = skey % A
    onehot = e_s[:, None] == experts[None, :]
    counts = jnp.sum(onehot, axis=0, dtype=jnp.int32)
    starts = jnp.cumsum(counts) - counts
    padded = (counts + blk - 1) // blk * blk
    pends = jnp.cumsum(padded)
    pstarts = pends - padded
    shift = jnp.sum(jnp.where(onehot, (pstarts - starts)[None, :], 0), axis=1, dtype=jnp.int32)
    dest = jnp.arange(A, dtype=jnp.int32) + shift
    n_blocks = -(-A // blk) + N_EXPERTS
    buf_tok = jnp.zeros((n_blocks * blk,), jnp.int32).at[dest].set(order // TOP_K)
    slot = lax.sort((order, dest), num_keys=1)[1].reshape(T, TOP_K)
    block_start = jnp.arange(n_blocks, dtype=jnp.int32) * blk
    block_exp = jnp.minimum(jnp.sum(pends[None, :] <= block_start[:, None], axis=1, dtype=jnp.int32),
                            N_EXPERTS - 1)
    n_used = (pends[-1] // blk).astype(jnp.int32).reshape(1)
    return buf_tok, slot, block_exp, n_used


def _prep_weights(w_in, w_uq, w_ukv):
    q0 = Q_LORA_RANK + KV_LORA_RANK
    d0 = q0 + QK_ROPE_DIM
    half = QK_ROPE_DIM // 2
    pad = jnp.zeros((D_MODEL, LANES - QK_ROPE_DIM), F32)
    k_r = w_in[:, q0:d0]
    k_r_sw = jnp.concatenate([k_r[:, half:], k_r[:, :half]], axis=1)
    w_mla = jnp.concatenate([w_in[:, :q0], k_r, pad, k_r_sw, pad], axis=1).astype(BF16)
    dil_cols = DIL_GROUPS * DIL_WIDTH
    qkv = [w_in[:, d0 + t * dil_cols:d0 + (t + 1) * dil_cols] for t in range(3)]
    gates = w_in[:, d0 + 3 * dil_cols:]
    group_w = [jnp.concatenate([m[:, g * DIL_WIDTH:(g + 1) * DIL_WIDTH] for m in qkv], axis=1)
               for g in range(DIL_GROUPS)]
    w_gate0 = jnp.concatenate([gates, group_w[0]], axis=1).astype(BF16)
    w_dil = [w.astype(BF16) for w in group_w[1:]]
    uq = w_uq.reshape(Q_LORA_RANK, MLA_HEADS, QK_NOPE_DIM + QK_ROPE_DIM)
    nope = uq[:, :, :QK_NOPE_DIM].reshape(Q_LORA_RANK, -1)
    rope = uq[:, :, QK_NOPE_DIM:]
    rope_sw = jnp.concatenate([rope[:, :, half:], rope[:, :, :half]], axis=2)
    zpad = jnp.zeros((Q_LORA_RANK, MLA_HEADS, LANES - QK_ROPE_DIM), F32)
    rope = jnp.concatenate([rope, zpad], axis=2).reshape(Q_LORA_RANK, -1)
    rope_sw = jnp.concatenate([rope_sw, zpad], axis=2).reshape(Q_LORA_RANK, -1)
    w_uq_ext = jnp.concatenate([nope, rope, rope_sw], axis=1).astype(BF16)
    ukv = w_ukv.reshape(KV_LORA_RANK, MLA_HEADS, QK_NOPE_DIM + V_HEAD_DIM)
    w_ukv_r = jnp.concatenate([ukv[:, :, :QK_NOPE_DIM].reshape(KV_LORA_RANK, -1),
                               ukv[:, :, QK_NOPE_DIM:].reshape(KV_LORA_RANK, -1)], axis=1).astype(BF16)
    return w_mla, w_gate0, w_dil, w_uq_ext, w_ukv_r


def _rope_tables(positions):
    half = QK_ROPE_DIM // 2
    inv_freq = 1.0 / (ROPE_THETA ** (jnp.arange(half, dtype=F32) / half))
    ang = positions.astype(F32)[..., None] * inv_freq
    cos, sin = jnp.cos(ang), jnp.sin(ang)
    zero = jnp.zeros(cos.shape[:-1] + (LANES - QK_ROPE_DIM,), F32)
    return (jnp.concatenate([cos, cos, zero], axis=-1),
            jnp.concatenate([-sin, sin, zero], axis=-1))


def kernel(x, c, positions, mod_w, mod_b, norm1_g, w_in, q_norm_g, w_uq, kv_norm_g, w_ukv, rel_bias,
           w_o_mla, w_o_dil, w_out, norm2_g, w_group, b_group, w_expert, b_expert, w_gate, w_up,
           w_down, final_g):
    B, S, D = x.shape
    T = B * S
    layer = 0
    x2d = x.reshape(T, D)
    mods = _mods(c, mod_w[layer], mod_b[layer])
    w_mla, w_gate0, w_dil, w_uq_ext, w_ukv_r = _prep_weights(w_in[layer], w_uq[layer], w_ukv[layer])
    cos_t, sin_t = _rope_tables(positions)

    proj_gate, h = _inproj_gate(x2d, mods, norm1_g[layer], w_gate0, S)
    qkv = [proj_gate.reshape(B, 1, S, -1)]
    qkv += [_inproj_dil(h, w_dil[g - 1], DIL_PAIRS[g][1], B, S) for g in range(1, DIL_GROUPS)]
    qt, k, vt = _mla_prep(h.reshape(B, S, D), w_mla, q_norm_g[layer], w_uq_ext,
                          kv_norm_g[layer], w_ukv_r, cos_t, sin_t)
    o_mla = _mla_attn(qt, k, vt).reshape(T, MLA_HEADS * V_HEAD_DIM)

    o_dil, lse_dil = [], []
    for g, (_, dil) in enumerate(DIL_PAIRS):
        qcol = GATE_COLS // COL_TILE if g == 0 else 0
        o_g, lse_g = _dil_attn(qkv[g], _dil_bias_tile(rel_bias, g, dil), dil, qcol)
        o_dil.append(o_g)
        lse_dil.append(lse_g)

    w_router = jnp.zeros((D, LANES), F32).at[:, :N_GROUPS].set(w_group[layer]) \
        .at[:, ROUTE_E0:ROUTE_E0 + N_EXPERTS].set(w_expert[layer])
    b_router = jnp.zeros((1, LANES), F32).at[0, :N_GROUPS].set(b_group[layer]) \
        .at[0, ROUTE_E0:ROUTE_E0 + N_EXPERTS].set(b_expert[layer])
    x1, h2, ids, wts = _outproj(x2d, o_mla, o_dil, lse_dil, proj_gate, mods,
                                w_o_mla[layer].astype(BF16), w_o_dil[layer].astype(BF16),
                                w_out[layer].astype(BF16), norm2_g[layer], w_router, b_router, S)

    moe_blk = 256
    buf_tok, slot, block_exp, n_used = _route_slots(ids[:, :TOP_K], moe_blk)
    x_buf = h2[buf_tok]
    y_buf = _moe_ffn(x_buf, block_exp, n_used, w_gate[layer].astype(BF16), w_up[layer].astype(BF16),
                     w_down[layer].astype(BF16), moe_blk)
    out = _final(x1, y_buf[slot[:, 0]], y_buf[slot[:, 1]], wts, mods, final_g, S)
    return out.reshape(B, S, D)
```

```python
import functools
import math

import jax
import jax.numpy as jnp
import numpy as np
from jax import lax
from jax.experimental import pallas as pl
from jax.experimental.pallas import tpu as pltpu

F32 = jnp.float32
BF16 = jnp.bfloat16

D_MODEL = 2048
MLA_HEADS = 8
Q_LORA_RANK = 512
KV_LORA_RANK = 512
QK_NOPE_DIM = 128
QK_ROPE_DIM = 64
V_HEAD_DIM = 128
ROPE_THETA = 10000.0
DIL_PAIRS = ((128, 1), (512, 4), (2048, 16))
DIL_GROUPS = len(DIL_PAIRS)
DIL_HEADS = 8
DIL_HEAD_DIM = 128
DIL_WIDTH = DIL_HEADS * DIL_HEAD_DIM
DIL_KEYS = 129
REL_BUCKETS = 32
REL_MAX_DISTANCE = 2048
N_GROUPS = 8
EXPERTS_PER_GROUP = 8
N_EXPERTS = N_GROUPS * EXPERTS_PER_GROUP
TOP_K = 2
EXPERT_FF = 1408
N_MODS = 6
EPS = 1e-6

LANES = 128
MLA_QK_PAD = 256
GATE_COLS = 2 * D_MODEL
COL_TILE = 1024
LSE_LANES = LANES // DIL_HEADS
ROUTE_E0 = N_GROUPS
MASK_VALUE = -1e30
VMEM_LIMIT = 56 * 1024 * 1024
LOG2E = math.log2(math.e)

SH1, SC1, GT1, SH2, SC2, GT2 = range(N_MODS)


def _params(sem, vmem=VMEM_LIMIT):
    return pltpu.CompilerParams(dimension_semantics=sem, vmem_limit_bytes=vmem)


def _norm_modulate(x, g, scale, shift):
    ms = jnp.mean(x * x, axis=-1, keepdims=True)
    return x * lax.rsqrt(ms + EPS) * g * (1.0 + scale) + shift


def _rms(x, g):
    return x * lax.rsqrt(jnp.mean(x * x, axis=-1, keepdims=True) + EPS) * g


def _mods_kernel(c_ref, w_ref, b_ref, o_ref):
    c = c_ref[...]
    act = c * jax.nn.sigmoid(c)
    o_ref[...] = jnp.dot(act, w_ref[...], preferred_element_type=F32,
                         precision=lax.Precision.HIGHEST) + b_ref[...]


def _mods(c, mod_w, mod_b):
    B = c.shape[0]
    rows = 8
    c_pad = jnp.zeros((rows, D_MODEL), F32).at[:B].set(c)
    n = mod_w.shape[1]
    tn = 1536
    out = pl.pallas_call(
        _mods_kernel,
        grid=(n // tn,),
        in_specs=[pl.BlockSpec((rows, D_MODEL), lambda j: (0, 0)),
                  pl.BlockSpec((D_MODEL, tn), lambda j: (0, j)),
                  pl.BlockSpec((1, tn), lambda j: (0, j))],
        out_specs=pl.BlockSpec((rows, tn), lambda j: (0, j)),
        out_shape=jax.ShapeDtypeStruct((rows, n), F32),
        compiler_params=_params(("arbitrary",)),
        name="mods",
    )(c_pad, mod_w, mod_b.reshape(1, n))
    return out[:B].reshape(B, N_MODS, D_MODEL)


def _inproj_gate_kernel(x_ref, mods_ref, g_ref, w_ref, o_ref, h_ref, *, gate_tiles):
    j = pl.program_id(1)

    @pl.when(j == 0)
    def _():
        h = _norm_modulate(x_ref[...], g_ref[...], mods_ref[0, SC1:SC1 + 1, :],
                           mods_ref[0, SH1:SH1 + 1, :])
        h_ref[...] = h.astype(BF16)

    acc = jnp.dot(h_ref[...], w_ref[...], preferred_element_type=F32)
    o_ref[...] = jnp.where(j < gate_tiles, jax.nn.sigmoid(acc), acc).astype(BF16)


def _inproj_gate(x2d, mods, norm_g, w, seq):
    T = x2d.shape[0]
    tm = 1024
    tiles_per_batch = seq // tm
    ncols = w.shape[1]
    return pl.pallas_call(
        functools.partial(_inproj_gate_kernel, gate_tiles=GATE_COLS // COL_TILE),
        grid=(T // tm, ncols // COL_TILE),
        in_specs=[pl.BlockSpec((tm, D_MODEL), lambda i, j: (i, 0)),
                  pl.BlockSpec((1, N_MODS, D_MODEL), lambda i, j: (i // tiles_per_batch, 0, 0)),
                  pl.BlockSpec((1, D_MODEL), lambda i, j: (0, 0)),
                  pl.BlockSpec((D_MODEL, COL_TILE), lambda i, j: (0, j))],
        out_specs=[pl.BlockSpec((tm, COL_TILE), lambda i, j: (i, j)),
                   pl.BlockSpec((tm, D_MODEL), lambda i, j: (i, 0))],
        out_shape=[jax.ShapeDtypeStruct((T, ncols), BF16),
                   jax.ShapeDtypeStruct((T, D_MODEL), BF16)],
        compiler_params=_params(("arbitrary", "arbitrary")),
        name="inproj_gate",
    )(x2d, mods, norm_g.reshape(1, D_MODEL), w)


def _inproj_dil_kernel(h_ref, w_ref, o_ref, scr, *, dil):
    acc = jnp.dot(h_ref[...], w_ref[...], preferred_element_type=F32)
    tm = acc.shape[0]
    n = tm // dil
    chunks = acc.shape[1] // LANES
    for c in range(chunks):
        scr[c] = acc[:, c * LANES:(c + 1) * LANES]
    for r in range(dil):
        for c in range(chunks):
            o_ref[0, r, :, c * LANES:(c + 1) * LANES] = scr[c, pl.ds(r, n, stride=dil), :].astype(BF16)


def _inproj_dil(h2d, w, dil, batch, seq):
    tm = 1024
    tiles_per_batch = seq // tm
    ncols = w.shape[1]
    return pl.pallas_call(
        functools.partial(_inproj_dil_kernel, dil=dil),
        grid=(batch * tiles_per_batch, ncols // COL_TILE),
        in_specs=[pl.BlockSpec((tm, D_MODEL), lambda i, j: (i, 0)),
                  pl.BlockSpec((D_MODEL, COL_TILE), lambda i, j: (0, j))],
        out_specs=pl.BlockSpec((1, dil, tm // dil, COL_TILE),
                               lambda i, j: (i // tiles_per_batch, 0, i % tiles_per_batch, j)),
        out_shape=jax.ShapeDtypeStruct((batch, dil, seq // dil, ncols), BF16),
        scratch_shapes=[pltpu.VMEM((COL_TILE // LANES, tm, LANES), F32)],
        compiler_params=_params(("arbitrary", "arbitrary")),
        name=f"inproj_dil{dil}",
    )(h2d, w)


def _mla_prep_kernel(h_ref, wm_ref, qg_ref, wuq_ref, kvg_ref, wukv_ref,
                     cos_ref, sin_ref, qt_ref, k_ref, vt_ref, *, scale):
    cm = jnp.dot(h_ref[0], wm_ref[...], preferred_element_type=F32)
    cos = cos_ref[0]
    sin = sin_ref[0]
    cq = _rms(cm[:, :Q_LORA_RANK], qg_ref[...]).astype(BF16)
    qx = jnp.dot(cq, wuq_ref[...], preferred_element_type=F32)
    ckv = _rms(cm[:, Q_LORA_RANK:Q_LORA_RANK + KV_LORA_RANK], kvg_ref[...]).astype(BF16)
    kvx = jnp.dot(ckv, wukv_ref[...], preferred_element_type=F32)
    kr0 = Q_LORA_RANK + KV_LORA_RANK
    k_rope = (cm[:, kr0:kr0 + LANES] * cos + cm[:, kr0 + LANES:kr0 + 2 * LANES] * sin).astype(BF16)
    hn = MLA_HEADS * QK_NOPE_DIM
    for hd in range(MLA_HEADS):
        lo = hd * LANES
        q_rope = qx[:, hn + lo:hn + lo + LANES] * cos + qx[:, 2 * hn + lo:2 * hn + lo + LANES] * sin
        qt_ref[0, hd, :LANES, :] = (qx[:, lo:lo + LANES] * scale).T.astype(BF16)
        qt_ref[0, hd, LANES:, :] = (q_rope * scale).T.astype(BF16)
        k_ref[0, hd, :, :LANES] = kvx[:, lo:lo + LANES].astype(BF16)
        k_ref[0, hd, :, LANES:] = k_rope
        vt_ref[0, hd] = kvx[:, hn + lo:hn + lo + LANES].T.astype(BF16)


def _mla_prep(h3d, w_mla, q_norm_g, w_uq_ext, kv_norm_g, w_ukv_r, cos_t, sin_t):
    B, S, _ = h3d.shape
    tm = 512
    const2 = lambda b, i: (0, 0)
    scale = (QK_NOPE_DIM + QK_ROPE_DIM) ** -0.5 * LOG2E
    return pl.pallas_call(
        functools.partial(_mla_prep_kernel, scale=scale),
        grid=(B, S // tm),
        in_specs=[pl.BlockSpec((1, tm, D_MODEL), lambda b, i: (b, i, 0)),
                  pl.BlockSpec(w_mla.shape, const2),
                  pl.BlockSpec((1, Q_LORA_RANK), const2),
                  pl.BlockSpec(w_uq_ext.shape, const2),
                  pl.BlockSpec((1, KV_LORA_RANK), const2),
                  pl.BlockSpec(w_ukv_r.shape, const2),
                  pl.BlockSpec((1, tm, LANES), lambda b, i: (b, i, 0)),
                  pl.BlockSpec((1, tm, LANES), lambda b, i: (b, i, 0))],
        out_specs=[pl.BlockSpec((1, MLA_HEADS, MLA_QK_PAD, tm), lambda b, i: (b, 0, 0, i)),
                   pl.BlockSpec((1, MLA_HEADS, tm, MLA_QK_PAD), lambda b, i: (b, 0, i, 0)),
                   pl.BlockSpec((1, MLA_HEADS, V_HEAD_DIM, tm), lambda b, i: (b, 0, 0, i))],
        out_shape=[jax.ShapeDtypeStruct((B, MLA_HEADS, MLA_QK_PAD, S), BF16),
                   jax.ShapeDtypeStruct((B, MLA_HEADS, S, MLA_QK_PAD), BF16),
                   jax.ShapeDtypeStruct((B, MLA_HEADS, V_HEAD_DIM, S), BF16)],
        compiler_params=_params(("arbitrary", "arbitrary")),
        name="mla_prep",
    )(h3d, w_mla, q_norm_g.reshape(1, -1), w_uq_ext, kv_norm_g.reshape(1, -1), w_ukv_r, cos_t, sin_t)


def _mla_attn_kernel(qt_ref, k_ref, vt_ref, o_ref, s_a, s_b, cmax_a, cmax_b, m_scr, l_scr, acc_scr,
                     *, tq, tk):
    i = pl.program_id(2)
    qt = qt_ref[0, 0]

    def scores(t, s_ref, cmax_ref, mask_off):
        start = pl.multiple_of(t * tk, tk)
        st = jnp.dot(k_ref[0, 0, pl.ds(start, tk), :], qt, preferred_element_type=F32)
        if mask_off is not None:
            krow = lax.broadcasted_iota(jnp.int32, (tk, tq), 0) + mask_off
            qcol = lax.broadcasted_iota(jnp.int32, (tk, tq), 1)
            st = jnp.where(krow <= qcol, st, MASK_VALUE)
        s_ref[...] = st
        cmax_ref[...] = jnp.max(st, axis=0, keepdims=True)

    def accumulate(t, s_ref, cmax_ref):
        start = pl.multiple_of(t * tk, tk)
        m = m_scr[...]
        m_new = jnp.maximum(m, cmax_ref[...])
        alpha = jnp.exp2(m - m_new)
        p = jnp.exp2(s_ref[...] - m_new)
        l_scr[...] = alpha * l_scr[...] + jnp.sum(p, axis=0, keepdims=True)
        vt = vt_ref[0, 0, :, pl.ds(start, tk)]
        acc_scr[...] = alpha * acc_scr[...] + jnp.dot(vt, p.astype(BF16), preferred_element_type=F32)
        m_scr[...] = m_new

    m_scr[...] = jnp.full(m_scr.shape, MASK_VALUE, F32)
    l_scr[...] = jnp.zeros(l_scr.shape, F32)
    acc_scr[...] = jnp.zeros(acc_scr.shape, F32)

    @pl.when(i == 0)
    def _():
        scores(0, s_a, cmax_a, 0)

    @pl.when(i > 0)
    def _():
        scores(0, s_a, cmax_a, None)

    def pair(p, carry):
        scores(2 * p + 1, s_b, cmax_b, None)
        accumulate(2 * p, s_a, cmax_a)
        scores(2 * p + 2, s_a, cmax_a, None)
        accumulate(2 * p + 1, s_b, cmax_b)
        return carry

    lax.fori_loop(0, i - 1, pair, 0)

    @pl.when(i > 0)
    def _():
        scores(2 * i - 1, s_b, cmax_b, None)
        accumulate(2 * i - 2, s_a, cmax_a)
        scores(2 * i, s_a, cmax_a, 0)
        accumulate(2 * i - 1, s_b, cmax_b)

    scores(2 * i + 1, s_b, cmax_b, tk)
    accumulate(2 * i, s_a, cmax_a)
    accumulate(2 * i + 1, s_b, cmax_b)
    o_ref[0] = (acc_scr[...] / l_scr[...]).T.astype(BF16)


def _mla_attn(qt, k, vt):
    B, H, _, S = qt.shape
    tq = 1024
    tk = tq // 2
    return pl.pallas_call(
        functools.partial(_mla_attn_kernel, tq=tq, tk=tk),
        grid=(B, H, S // tq),
        in_specs=[pl.BlockSpec((1, 1, MLA_QK_PAD, tq), lambda b, h, i: (b, h, 0, i)),
                  pl.BlockSpec((1, 1, S, MLA_QK_PAD), lambda b, h, i: (b, h, 0, 0)),
                  pl.BlockSpec((1, 1, V_HEAD_DIM, S), lambda b, h, i: (b, h, 0, 0))],
        out_specs=pl.BlockSpec((1, tq, V_HEAD_DIM), lambda b, h, i: (b, i, h)),
        out_shape=jax.ShapeDtypeStruct((B, S, H * V_HEAD_DIM), BF16),
        scratch_shapes=[pltpu.VMEM((tk, tq), F32), pltpu.VMEM((tk, tq), F32),
                        pltpu.VMEM((1, tq), F32), pltpu.VMEM((1, tq), F32),
                        pltpu.VMEM((1, tq), F32), pltpu.VMEM((1, tq), F32),
                        pltpu.VMEM((V_HEAD_DIM, tq), F32)],
        compiler_params=_params(("arbitrary", "arbitrary", "arbitrary")),
        name="mla_attn",
    )(qt, k, vt)


_NT = (((1,), (1,)), ((), ()))


def _dil_attn_kernel(q_ref, kc_ref, kp_ref, vc_ref, vp_ref, bias_ref, o_ref, lse_ref, *, blk, scale):
    n = pl.program_id(2)
    col = lax.broadcasted_iota(jnp.int32, (blk, 2 * blk), 1)
    prev_ok = jnp.logical_or(col >= blk, n > 0)
    head_of_lane = lax.broadcasted_iota(jnp.int32, (blk, LANES), 1) // LSE_LANES
    lse_tile = jnp.zeros((blk, LANES), F32)
    for hd in range(DIL_HEADS):
        sl = slice(hd * DIL_HEAD_DIM, (hd + 1) * DIL_HEAD_DIM)
        q = q_ref[0, 0, :, sl]
        k = jnp.concatenate([kp_ref[0, 0, :, sl], kc_ref[0, 0, :, sl]], axis=0)
        v = jnp.concatenate([vp_ref[0, 0, :, sl], vc_ref[0, 0, :, sl]], axis=0)
        s = lax.dot_general(q, k, _NT, preferred_element_type=F32) * scale + bias_ref[hd]
        s = jnp.where(prev_ok, s, MASK_VALUE)
        m = jnp.max(s, axis=-1, keepdims=True)
        p = jnp.exp(s - m)
        den = jnp.sum(p, axis=-1, keepdims=True)
        o = jnp.dot(p.astype(BF16), v, preferred_element_type=F32) / den
        o_ref[0, 0, :, sl] = o.astype(BF16)
        lse_tile = jnp.where(head_of_lane == hd, m + jnp.log(den), lse_tile)
    lse_ref[0, 0] = lse_tile


def _dil_attn(qkv, bias_tile, dil, qcol):
    batch, _, sub, _ = qkv.shape
    blk = 128

    def cur(c):
        return pl.BlockSpec((1, 1, blk, DIL_WIDTH), lambda b, r, n: (b, r, n, c))

    def prev(c):
        return pl.BlockSpec((1, 1, blk, DIL_WIDTH), lambda b, r, n: (b, r, jnp.maximum(n - 1, 0), c))

    return pl.pallas_call(
        functools.partial(_dil_attn_kernel, blk=blk, scale=DIL_HEAD_DIM ** -0.5),
        grid=(batch, dil, sub // blk),
        in_specs=[cur(qcol), cur(qcol + 1), prev(qcol + 1), cur(qcol + 2), prev(qcol + 2),
                  pl.BlockSpec((DIL_HEADS, blk, 2 * blk), lambda b, r, n: (0, 0, 0))],
        out_specs=[pl.BlockSpec((1, 1, blk, DIL_WIDTH), lambda b, r, n: (b, r, n, 0)),
                   pl.BlockSpec((1, 1, blk, LANES), lambda b, r, n: (b, r, n, 0))],
        out_shape=[jax.ShapeDtypeStruct((batch, dil, sub, DIL_WIDTH), BF16),
                   jax.ShapeDtypeStruct((batch, dil, sub, LANES), F32)],
        compiler_params=_params(("arbitrary", "arbitrary", "arbitrary")),
        name=f"dil_attn{dil}",
    )(qkv, qkv, qkv, qkv, qkv, bias_tile)


def _t5_bucket(n):
    exact = REL_BUCKETS // 2
    nf = jnp.maximum(n, 1).astype(F32)
    large = exact + (jnp.log(nf / exact) / math.log(REL_MAX_DISTANCE / exact)
                     * (REL_BUCKETS - exact)).astype(jnp.int32)
    large = jnp.minimum(large, REL_BUCKETS - 1)
    return jnp.where(n < exact, n, large)


def _dil_bias_tile(rel_bias, group, dil, blk=128):
    offs = dil * jnp.arange(DIL_KEYS, dtype=jnp.int32)
    per_key = rel_bias[_t5_bucket(offs)][:, group * DIL_HEADS:(group + 1) * DIL_HEADS].astype(F32)
    L = 3 * blk
    u = jnp.full((DIL_HEADS, L), MASK_VALUE, F32).at[:, :DIL_KEYS].set(per_key[::-1].T)
    rows = jnp.tile(u, (1, blk))[:, :blk * (L - 1)].reshape(DIL_HEADS, blk, L - 1)
    return rows[:, :, :2 * blk]


def _route_topk(logits):
    lane = lax.broadcasted_iota(jnp.int32, logits.shape, 1)
    neg = -jnp.inf

    def first_argmax(vals):
        best = jnp.max(vals, axis=-1, keepdims=True)
        idx = jnp.min(jnp.where(vals == best, lane, LANES), axis=-1, keepdims=True)
        return best, idx

    is_group = lane < N_GROUPS
    g_best, g_idx = first_argmax(jnp.where(is_group, logits, neg))
    p_group = 1.0 / jnp.sum(jnp.where(is_group, jnp.exp(logits - g_best), 0.0), axis=-1, keepdims=True)
    e_lo = ROUTE_E0 + g_idx * EXPERTS_PER_GROUP
    in_group = jnp.logical_and(lane >= e_lo, lane < e_lo + EXPERTS_PER_GROUP)
    e_vals = jnp.where(in_group, logits, neg)
    v1, i1 = first_argmax(e_vals)
    v2, i2 = first_argmax(jnp.where(lane == i1, neg, e_vals))
    e21 = jnp.exp(v2 - v1)
    w1 = p_group / (1.0 + e21)
    w2 = p_group * e21 / (1.0 + e21)
    ids = jnp.where(lane == 0, i1 - ROUTE_E0, jnp.where(lane == 1, i2 - ROUTE_E0, 0))
    wts = jnp.where(lane == 0, w1, jnp.where(lane == 1, w2, 0.0))
    return ids, wts


def _outproj_kernel(x_ref, oa_ref, ob0_ref, ob1_ref, ob2_ref, l0_ref, l1_ref, l2_ref, ga_ref, gb_ref,
                    mods_ref, woa_ref, wob_ref, wout_ref, g2_ref, wr_ref, br_ref,
                    x1_ref, h2_ref, ids_ref, wts_ref, og_scr, lse_scr):
    tm = x_ref.shape[0]
    chunks = DIL_WIDTH // LANES
    for gi, (o_ref, l_ref) in enumerate(((ob1_ref, l1_ref), (ob2_ref, l2_ref))):
        dil = DIL_PAIRS[gi + 1][1]
        n = tm // dil
        for r in range(dil):
            lse_scr[gi, pl.ds(r, n, stride=dil), :] = l_ref[0, r]
            for c in range(chunks):
                og_scr[gi, c, pl.ds(r, n, stride=dil), :] = \
                    o_ref[0, r, :, c * LANES:(c + 1) * LANES].astype(F32)
    ls = [l0_ref[0, 0], lse_scr[0], lse_scr[1]]
    m = functools.reduce(jnp.maximum, ls)
    es = [jnp.exp(l - m) for l in ls]
    tot = functools.reduce(lambda a, b: a + b, es)
    ws = [e / tot for e in es]
    pieces = []
    for hd in range(DIL_HEADS):
        sl = slice(hd * DIL_HEAD_DIM, (hd + 1) * DIL_HEAD_DIM)
        wl = hd * LSE_LANES
        acc = ws[0][:, wl:wl + 1] * ob0_ref[0, 0, :, sl].astype(F32)
        acc = acc + ws[1][:, wl:wl + 1] * og_scr[0, hd]
        acc = acc + ws[2][:, wl:wl + 1] * og_scr[1, hd]
        pieces.append(acc.astype(BF16))
    ob = jnp.concatenate(pieces, axis=1)
    o_a = jnp.dot(oa_ref[...], woa_ref[...], preferred_element_type=F32)
    o_b = jnp.dot(ob, wob_ref[...], preferred_element_type=F32)
    merged = ga_ref[...].astype(F32) * o_a + gb_ref[...].astype(F32) * o_b
    mix = jnp.dot(merged.astype(BF16), wout_ref[...], preferred_element_type=F32)
    x1 = x_ref[...] + mods_ref[0, GT1:GT1 + 1, :] * mix
    x1_ref[...] = x1
    h2 = _norm_modulate(x1, g2_ref[...], mods_ref[0, SC2:SC2 + 1, :], mods_ref[0, SH2:SH2 + 1, :])
    h2_ref[...] = h2.astype(BF16)
    h2_hi = h2.astype(BF16)
    h2_lo = (h2 - h2_hi.astype(F32)).astype(BF16)
    hi_both = jnp.dot(h2_hi, wr_ref[...], preferred_element_type=F32)
    lo_hi = jnp.dot(h2_lo, wr_ref[:, :LANES], preferred_element_type=F32)
    logits = hi_both[:, :LANES] + hi_both[:, LANES:] + lo_hi + br_ref[...]
    ids, wts = _route_topk(logits)
    ids_ref[...] = ids
    wts_ref[...] = wts


def _outproj(x2d, o_mla, o_dil, lse_dil, proj_gate, mods, w_oa, w_ob, w_o, norm2_g, w_router, b_router, seq):
    T = x2d.shape[0]
    tm = 256
    tpb = seq // tm
    row = lambda i: (i, 0)
    const = lambda i: (0, 0)
    one_buf = pl.Buffered(1)

    def deint(width, dil):
        return pl.BlockSpec((1, dil, tm // dil, width), lambda i: (i // tpb, 0, i % tpb, 0))

    dils = [d for _, d in DIL_PAIRS]
    return pl.pallas_call(
        _outproj_kernel,
        grid=(T // tm,),
        in_specs=[pl.BlockSpec((tm, D_MODEL), row),
                  pl.BlockSpec((tm, DIL_WIDTH), row),
                  deint(DIL_WIDTH, dils[0]), deint(DIL_WIDTH, dils[1]), deint(DIL_WIDTH, dils[2]),
                  deint(LANES, dils[0]), deint(LANES, dils[1]), deint(LANES, dils[2]),
                  pl.BlockSpec((tm, D_MODEL), lambda i: (i, 0)),
                  pl.BlockSpec((tm, D_MODEL), lambda i: (i, 1)),
                  pl.BlockSpec((1, N_MODS, D_MODEL), lambda i: (i // tpb, 0, 0)),
                  pl.BlockSpec(w_oa.shape, const, pipeline_mode=one_buf),
                  pl.BlockSpec(w_ob.shape, const, pipeline_mode=one_buf),
                  pl.BlockSpec(w_o.shape, const, pipeline_mode=one_buf),
                  pl.BlockSpec((1, D_MODEL), const),
                  pl.BlockSpec(w_router.shape, const, pipeline_mode=one_buf),
                  pl.BlockSpec((1, LANES), const)],
        out_specs=[pl.BlockSpec((tm, D_MODEL), row),
                   pl.BlockSpec((tm, D_MODEL), row),
                   pl.BlockSpec((tm, LANES), row),
                   pl.BlockSpec((tm, LANES), row)],
        out_shape=[jax.ShapeDtypeStruct((T, D_MODEL), F32),
                   jax.ShapeDtypeStruct((T, D_MODEL), BF16),
                   jax.ShapeDtypeStruct((T, LANES), jnp.int32),
                   jax.ShapeDtypeStruct((T, LANES), F32)],
        scratch_shapes=[pltpu.VMEM((DIL_GROUPS - 1, DIL_WIDTH // LANES, tm, LANES), F32),
                        pltpu.VMEM((DIL_GROUPS - 1, tm, LANES), F32)],
        compiler_params=_params(("arbitrary",)),
        name="outproj",
    )(x2d, o_mla, o_dil[0], o_dil[1], o_dil[2], lse_dil[0], lse_dil[1], lse_dil[2],
      proj_gate, proj_gate, mods, w_oa, w_ob, w_o, norm2_g.reshape(1, D_MODEL), w_router, b_router)


def _moe_kernel(be_ref, nused_ref, x_ref, wg_ref, wu_ref, wd_ref, o_ref):
    i = pl.program_id(0)

    @pl.when(i < nused_ref[0])
    def _():
        x = x_ref[...]
        g = jnp.dot(x, wg_ref[0], preferred_element_type=F32)
        u = jnp.dot(x, wu_ref[0], preferred_element_type=F32)
        a = (g * jax.nn.sigmoid(g) * u).astype(BF16)
        o_ref[...] = jnp.dot(a, wd_ref[0], preferred_element_type=F32).astype(BF16)

    @pl.when(i >= nused_ref[0])
    def _():
        o_ref[...] = jnp.zeros_like(o_ref)


def _moe_ffn(x_buf, block_exp, n_used, w_gate, w_up, w_down, blk):
    P = x_buf.shape[0]
    wsel = lambda i, be, nu: (be[i], 0, 0)
    row = lambda i, be, nu: (i, 0)
    return pl.pallas_call(
        _moe_kernel,
        grid_spec=pltpu.PrefetchScalarGridSpec(
            num_scalar_prefetch=2,
            grid=(P // blk,),
            in_specs=[pl.BlockSpec((blk, D_MODEL), row),
                      pl.BlockSpec((1, D_MODEL, EXPERT_FF), wsel),
                      pl.BlockSpec((1, D_MODEL, EXPERT_FF), wsel),
                      pl.BlockSpec((1, EXPERT_FF, D_MODEL), wsel)],
            out_specs=pl.BlockSpec((blk, D_MODEL), row)),
        out_shape=jax.ShapeDtypeStruct((P, D_MODEL), BF16),
        compiler_params=_params(("arbitrary",)),
        name="moe_ffn",
    )(block_exp, n_used, x_buf, w_gate, w_up, w_down)


def _final_kernel(x1_ref, y0_ref, y1_ref, w_ref, mods_ref, g_ref, o_ref):
    w = w_ref[...]
    y = w[:, 0:1] * y0_ref[...].astype(F32) + w[:, 1:2] * y1_ref[...].astype(F32)
    x2 = x1_ref[...] + mods_ref[0, GT2:GT2 + 1, :] * y
    o_ref[...] = _rms(x2, g_ref[...])


def _final(x1, y0, y1, weights, mods, final_g, seq):
    T = x1.shape[0]
    tm = 512
    tiles_per_batch = seq // tm
    row = lambda i: (i, 0)
    return pl.pallas_call(
        _final_kernel,
        grid=(T // tm,),
        in_specs=[pl.BlockSpec((tm, D_MODEL), row),
                  pl.BlockSpec((tm, D_MODEL), row),
                  pl.BlockSpec((tm, D_MODEL), row),
                  pl.BlockSpec((tm, LANES), row),
                  pl.BlockSpec((1, N_MODS, D_MODEL), lambda i: (i // tiles_per_batch, 0, 0)),
                  pl.BlockSpec((1, D_MODEL), lambda i: (0, 0))],
        out_specs=pl.BlockSpec((tm, D_MODEL), row),
        out_shape=jax.ShapeDtypeStruct((T, D_MODEL), F32),
        compiler_params=_params(("arbitrary",)),
        name="final",
    )(x1, y0, y1, weights, mods, final_g.reshape(1, D_MODEL))


def _route_slots(expert_id, blk):
    T = expert_id.shape[0]
    A = T * TOP_K
    experts = jnp.arange(N_EXPERTS, dtype=jnp.int32)
    key = expert_id.reshape(A) * A + jnp.arange(A, dtype=jnp.int32)
    skey = jnp.sort(key)
    e_s = skey // A
    order = skey % A
    onehot = e_s[:, None] == experts[None, :]
    counts = jnp.sum(onehot, axis=0, dtype=jnp.int32)
    starts = jnp.cumsum(counts) - counts
    padded = (counts + blk - 1) // blk * blk
    pends = jnp.cumsum(padded)
    pstarts = pends - padded
    shift = jnp.sum(jnp.where(onehot, (pstarts - starts)[None, :], 0), axis=1, dtype=jnp.int32)
    dest = jnp.arange(A, dtype=jnp.int32) + shift
    n_blocks = -(-A // blk) + N_EXPERTS
    buf_tok = jnp.zeros((n_blocks * blk,), jnp.int32).at[dest].set(order // TOP_K)
    slot = lax.sort((order, dest), num_keys=1)[1].reshape(T, TOP_K)
    block_start = jnp.arange(n_blocks, dtype=jnp.int32) * blk
    block_exp = jnp.minimum(jnp.sum(pends[None, :] <= block_start[:, None], axis=1, dtype=jnp.int32),
                            N_EXPERTS - 1)
    n_used = (pends[-1] // blk).astype(jnp.int32).reshape(1)
    return buf_tok, slot, block_exp, n_used


def _prep_weights(w_in, w_uq, w_ukv):
    q0 = Q_LORA_RANK + KV_LORA_RANK
    d0 = q0 + QK_ROPE_DIM
    half = QK_ROPE_DIM // 2
    pad = jnp.zeros((D_MODEL, LANES - QK_ROPE_DIM), F32)
    k_r = w_in[:, q0:d0]
    k_r_sw = jnp.concatenate([k_r[:, half:], k_r[:, :half]], axis=1)
    w_mla = jnp.concatenate([w_in[:, :q0], k_r, pad, k_r_sw, pad], axis=1).astype(BF16)
    dil_cols = DIL_GROUPS * DIL_WIDTH
    qkv = [w_in[:, d0 + t * dil_cols:d0 + (t + 1) * dil_cols] for t in range(3)]
    gates = w_in[:, d0 + 3 * dil_cols:]
    group_w = [jnp.concatenate([m[:, g * DIL_WIDTH:(g + 1) * DIL_WIDTH] for m in qkv], axis=1)
               for g in range(DIL_GROUPS)]
    w_gate0 = jnp.concatenate([gates, group_w[0]], axis=1).astype(BF16)
    w_dil = [w.astype(BF16) for w in group_w[1:]]
    uq = w_uq.reshape(Q_LORA_RANK, MLA_HEADS, QK_NOPE_DIM + QK_ROPE_DIM)
    nope = uq[:, :, :QK_NOPE_DIM].reshape(Q_LORA_RANK, -1)
    rope = uq[:, :, QK_NOPE_DIM:]
    rope_sw = jnp.concatenate([rope[:, :, half:], rope[:, :, :half]], axis=2)
    zpad = jnp.zeros((Q_LORA_RANK, MLA_HEADS, LANES - QK_ROPE_DIM), F32)
    rope = jnp.concatenate([rope, zpad], axis=2).reshape(Q_LORA_RANK, -1)
    rope_sw = jnp.concatenate([rope_sw, zpad], axis=2).reshape(Q_LORA_RANK, -1)
    w_uq_ext = jnp.concatenate([nope, rope, rope_sw], axis=1).astype(BF16)
    ukv = w_ukv.reshape(KV_LORA_RANK, MLA_HEADS, QK_NOPE_DIM + V_HEAD_DIM)
    w_ukv_r = jnp.concatenate([ukv[:, :, :QK_NOPE_DIM].reshape(KV_LORA_RANK, -1),
                               ukv[:, :, QK_NOPE_DIM:].reshape(KV_LORA_RANK, -1)], axis=1).astype(BF16)
    return w_mla, w_gate0, w_dil, w_uq_ext, w_ukv_r


def _rope_tables(positions):
    half = QK_ROPE_DIM // 2
    inv_freq = 1.0 / (ROPE_THETA ** (jnp.arange(half, dtype=F32) / half))
    ang = positions.astype(F32)[..., None] * inv_freq
    cos, sin = jnp.cos(ang), jnp.sin(ang)
    zero = jnp.zeros(cos.shape[:-1] + (LANES - QK_ROPE_DIM,), F32)
    return (jnp.concatenate([cos, cos, zero], axis=-1),
            jnp.concatenate([-sin, sin, zero], axis=-1))


def kernel(x, c, positions, mod_w, mod_b, norm1_g, w_in, q_norm_g, w_uq, kv_norm_g, w_ukv, rel_bias,
           w_o_mla, w_o_dil, w_out, norm2_g, w_group, b_group, w_expert, b_expert, w_gate, w_up,
           w_down, final_g):
    B, S, D = x.shape
    T = B * S
    layer = 0
    x2d = x.reshape(T, D)
    mods = _mods(c, mod_w[layer], mod_b[layer])
    w_mla, w_gate0, w_dil, w_uq_ext, w_ukv_r = _prep_weights(w_in[layer], w_uq[layer], w_ukv[layer])
    cos_t, sin_t = _rope_tables(positions)

    proj_gate, h = _inproj_gate(x2d, mods, norm1_g[layer], w_gate0, S)
    qkv = [proj_gate.reshape(B, 1, S, -1)]
    qkv += [_inproj_dil(h, w_dil[g - 1], DIL_PAIRS[g][1], B, S) for g in range(1, DIL_GROUPS)]
    qt, k, vt = _mla_prep(h.reshape(B, S, D), w_mla, q_norm_g[layer], w_uq_ext,
                          kv_norm_g[layer], w_ukv_r, cos_t, sin_t)
    o_mla = _mla_attn(qt, k, vt).reshape(T, MLA_HEADS * V_HEAD_DIM)

    o_dil, lse_dil = [], []
    for g, (_, dil) in enumerate(DIL_PAIRS):
        qcol = GATE_COLS // COL_TILE if g == 0 else 0
        o_g, lse_g = _dil_attn(qkv[g], _dil_bias_tile(rel_bias, g, dil), dil, qcol)
        o_dil.append(o_g)
        lse_dil.append(lse_g)

    w_router = jnp.zeros((D, LANES), F32).at[:, :N_GROUPS].set(w_group[layer]) \
        .at[:, ROUTE_E0:ROUTE_E0 + N_EXPERTS].set(w_expert[layer])
    b_router = jnp.zeros((1, LANES), F32).at[0, :N_GROUPS].set(b_group[layer]) \
        .at[0, ROUTE_E0:ROUTE_E0 + N_EXPERTS].set(b_expert[layer])
    w_router_hi = w_router.astype(BF16)
    w_router_lo = (w_router - w_router_hi.astype(F32)).astype(BF16)
    w_router2 = jnp.concatenate([w_router_hi, w_router_lo], axis=1)
    x1, h2, ids, wts = _outproj(x2d, o_mla, o_dil, lse_dil, proj_gate, mods,
                                w_o_mla[layer].astype(BF16), w_o_dil[layer].astype(BF16),
                                w_out[layer].astype(BF16), norm2_g[layer], w_router2, b_router, S)

    moe_blk = 256
    buf_tok, slot, block_exp, n_used = _route_slots(ids[:, :TOP_K], moe_blk)
    x_buf = h2[buf_tok]
    y_buf = _moe_ffn(x_buf, block_exp, n_used, w_gate[layer].astype(BF16), w_up[layer].astype(BF16),
                     w_down[layer].astype(BF16), moe_blk)
    out = _final(x1, y_buf[slot[:, 0]], y_buf[slot[:, 1]], wts, mods, final_g, S)
    return out.reshape(B, S, D)
```
